```python
import math
import jax, jax.numpy as jnp
from jax import lax
import numpy as np

D_MODEL = 2048
BATCH = 4
SEQ = 2048
DEPTH = 1
DEC_BATCH = 32
DEC_SEQ = 8
PAST_LEN = 8192
PAGE_SIZE = 128

N_META = 16
Q_BLOCK = 128
H_DIFF = 8
HD_DIFF = 64
W_DIFF = H_DIFF * 2 * HD_DIFF
H_SB = 16
HD_SB = 64
W_SB = H_SB * HD_SB
D_FF = 4 * D_MODEL
N_BUCKETS = 32
MAX_DISTANCE = 128
NORM_EPS = 1e-6
NEG_INF = -1e30
W_IN_COLS = 3 * W_DIFF + 3 * W_SB + 2 * D_MODEL

kernel_name = 'hybrid_diffattn_stickbreak_decoder_step'


def rms_norm(x, g):
    xf = x.astype(jnp.float32)
    y = xf * lax.rsqrt(jnp.mean(xf * xf, axis=-1, keepdims=True) + NORM_EPS)
    return (y * g.astype(jnp.float32)).astype(x.dtype)


def rel_bucket(dist):
    n = jnp.maximum(dist, 0)
    max_exact = N_BUCKETS // 2
    nf = jnp.maximum(n, 1).astype(jnp.float32)
    large = max_exact + (jnp.log(nf / max_exact) / math.log(MAX_DISTANCE / max_exact)
                         * (N_BUCKETS - max_exact)).astype(jnp.int32)
    large = jnp.minimum(large, N_BUCKETS - 1)
    return jnp.where(n < max_exact, n, large)


def project(h, norm_g, w_in, b_gate, g_q, g_k):
    xn = rms_norm(h, norm_g)
    z = xn @ w_in
    splits = [W_DIFF, 2 * W_DIFF, 3 * W_DIFF, 3 * W_DIFF + W_SB, 3 * W_DIFF + 2 * W_SB,
              3 * W_DIFF + 3 * W_SB, 3 * W_DIFF + 3 * W_SB + D_MODEL]
    qd, kd, vd, qs, ks, vs, ga, gb = jnp.split(z, splits, axis=-1)
    lead = h.shape[:-1]
    qd = rms_norm(qd.reshape(*lead, H_DIFF, 2, HD_DIFF), g_q).reshape(*lead, H_DIFF, 2 * HD_DIFF)
    kd = rms_norm(kd.reshape(*lead, H_DIFF, 2, HD_DIFF), g_k).reshape(*lead, H_DIFF, 2 * HD_DIFF)
    vd = vd.reshape(*lead, H_DIFF, 2 * HD_DIFF)
    qs = qs.reshape(*lead, H_SB, HD_SB)
    ks = ks.reshape(*lead, H_SB, HD_SB)
    vs = vs.reshape(*lead, H_SB, HD_SB)
    gate_a = jax.nn.sigmoid(ga + b_gate[:D_MODEL])
    gate_b = jax.nn.sigmoid(gb + b_gate[D_MODEL:])
    return qd, kd, vd, qs, ks, vs, gate_a, gate_b


def attend(qd, qs, qpos, kd, vd, ks, vs, kpos, kvalid, rel_bias, lam, lam_init, subln_g):
    f32 = jnp.float32
    dist = qpos[:, None] - kpos[None, :]
    bias = jnp.transpose(rel_bias[rel_bucket(dist)], (2, 0, 1)).astype(f32)
    causal = (dist >= 0) & kvalid[None, :]
    scale = HD_DIFF ** -0.5

    def softmax_map(qa, ka):
        s = jnp.einsum('bqhd,bkhd->bhqk', qa, ka).astype(f32) * scale + bias
        return jax.nn.softmax(jnp.where(causal, s, NEG_INF), axis=-1)

    attn = (softmax_map(qd[..., :HD_DIFF], kd[..., :HD_DIFF])
            - lam * softmax_map(qd[..., HD_DIFF:], kd[..., HD_DIFF:]))
    od = jnp.einsum('bhqk,bkhe->bqhe', attn.astype(vd.dtype), vd)
    od = rms_norm(od, subln_g) * (1.0 - lam_init)
    strict = (dist > 0) & kvalid[None, :]
    zs = jnp.einsum('bqhd,bkhd->bhqk', qs, ks).astype(f32) * (HD_SB ** -0.5)
    log1m = jnp.where(strict, -jax.nn.softplus(zs), 0.0)
    after = lax.cumsum(log1m, axis=3, reverse=True) - log1m
    weights = jnp.where(strict, jnp.exp(jax.nn.log_sigmoid(zs) + after), 0.0)
    osb = jnp.einsum('bhqk,bkhd->bqhd', weights.astype(vs.dtype), vs)
    return od, osb


def mix_and_mlp(h, od, osb, gate_a, gate_b, w_br_a, w_br_b, w_o, norm2_g, w_up, w_down):
    lead = h.shape[:-1]
    ya = od.reshape(*lead, W_DIFF) @ w_br_a
    yb = osb.reshape(*lead, W_SB) @ w_br_b
    h = h + (gate_a * ya + gate_b * yb) @ w_o
    u = jax.nn.relu(rms_norm(h, norm2_g) @ w_up)
    return h + (u * u) @ w_down


def setup_inputs(seed: int = 0) -> dict:
    key = jax.random.key(seed)
    k = jax.random.split(key, 26)
    f32 = jnp.float32
    n_pages = PAST_LEN // PAGE_SIZE
    n_used = DEC_BATCH * n_pages
    n_phys = n_used + max(1, n_used // 4)

    def nrm(kk, shape, s):
        return jax.random.normal(kk, shape, f32) * s

    page_table = jax.random.permutation(k[6], n_phys)[:n_used].reshape(DEC_BATCH, n_pages).astype(jnp.int32)
    return {
        'x_prompt': nrm(k[0], (BATCH, SEQ, D_MODEL), 1.0),
        'x_sample': nrm(k[1], (DEC_BATCH, DEC_SEQ, D_MODEL), 1.0),
        'cache_k_diff': nrm(k[2], (DEPTH, n_phys, PAGE_SIZE, H_DIFF, 2 * HD_DIFF), 1.0),
        'cache_v_diff': nrm(k[3], (DEPTH, n_phys, PAGE_SIZE, H_DIFF, 2 * HD_DIFF), 1.0),
        'cache_k_sb': nrm(k[4], (DEPTH, n_phys, PAGE_SIZE, H_SB, HD_SB), 1.0),
        'cache_v_sb': nrm(k[5], (DEPTH, n_phys, PAGE_SIZE, H_SB, HD_SB), 1.0),
        'page_table': page_table,
        'meta_tokens': nrm(k[7], (N_META, D_MODEL), 1.0),
        'rel_bias': nrm(k[8], (N_BUCKETS, H_DIFF), 0.5),
        'norm1_g': 1.0 + nrm(k[9], (DEPTH, D_MODEL), 0.05),
        'w_in': nrm(k[10], (DEPTH, D_MODEL, W_IN_COLS), D_MODEL ** -0.5),
        'b_gate': nrm(k[11], (DEPTH, 2 * D_MODEL), 0.1),
        'qk_norm_q': 1.0 + nrm(k[12], (DEPTH, HD_DIFF), 0.05),
        'qk_norm_k': 1.0 + nrm(k[13], (DEPTH, HD_DIFF), 0.05),
        'lam_q1': nrm(k[14], (DEPTH, HD_DIFF), 0.1),
        'lam_k1': nrm(k[15], (DEPTH, HD_DIFF), 0.1),
        'lam_q2': nrm(k[16], (DEPTH, HD_DIFF), 0.1),
        'lam_k2': nrm(k[17], (DEPTH, HD_DIFF), 0.1),
        'subln_g': 1.0 + nrm(k[18], (DEPTH, 2 * HD_DIFF), 0.05),
        'w_branch_a': nrm(k[19], (DEPTH, W_DIFF, D_MODEL), W_DIFF ** -0.5),
        'w_branch_b': nrm(k[20], (DEPTH, W_SB, D_MODEL), W_SB ** -0.5),
        'w_o': nrm(k[21], (DEPTH, D_MODEL, D_MODEL), D_MODEL ** -0.5),
        'norm2_g': 1.0 + nrm(k[22], (DEPTH, D_MODEL), 0.05),
        'w_up': nrm(k[23], (DEPTH, D_MODEL, D_FF), D_MODEL ** -0.5),
        'w_down': nrm(k[24], (DEPTH, D_FF, D_MODEL), D_FF ** -0.5),
    }


def reference(x_prompt, x_sample, cache_k_diff, cache_v_diff, cache_k_sb, cache_v_sb, page_table,
              meta_tokens, rel_bias, norm1_g, w_in, b_gate, qk_norm_q, qk_norm_k,
              lam_q1, lam_k1, lam_q2, lam_k2, subln_g, w_branch_a, w_branch_b, w_o,
              norm2_g, w_up, w_down):
    f32 = jnp.float32
    n_pages = PAST_LEN // PAGE_SIZE
    pad = (-N_META) % Q_BLOCK
    bsz, seq, _ = x_prompt.shape
    dec_b, dec_t, _ = x_sample.shape
    p_len = pad + N_META + seq
    n_blocks = p_len // Q_BLOCK

    meta = jnp.broadcast_to(meta_tokens.astype(x_prompt.dtype)[None], (bsz, N_META, D_MODEL))
    hp = jnp.concatenate([jnp.zeros((bsz, pad, D_MODEL), x_prompt.dtype), meta, x_prompt], axis=1)
    pos_p = jnp.arange(p_len, dtype=jnp.int32) - pad
    valid_p = pos_p >= 0

    hs = x_sample
    kpos_s = jnp.arange(PAST_LEN + dec_t, dtype=jnp.int32)
    qpos_s = kpos_s[PAST_LEN:]
    valid_s = jnp.ones((PAST_LEN + dec_t,), dtype=bool)

    def to_blocks(a):
        return jnp.moveaxis(a.reshape(bsz, n_blocks, Q_BLOCK, *a.shape[2:]), 1, 0)

    def from_blocks(a):
        return jnp.moveaxis(a, 0, 1).reshape(bsz, p_len, *a.shape[3:])

    def gather_past(cache_l, new_rows):
        past = cache_l[page_table].reshape(dec_b, n_pages * PAGE_SIZE, *cache_l.shape[2:])
        return jnp.concatenate([past.astype(new_rows.dtype), new_rows], axis=1)

    kdp, vdp, ksp, vsp, kds, vds, kss, vss = [], [], [], [], [], [], [], []
    for l in range(DEPTH):
        lam_init = 0.8 - 0.6 * math.exp(-0.3 * l)
        lam = (jnp.exp(jnp.sum(lam_q1[l].astype(f32) * lam_k1[l].astype(f32)))
               - jnp.exp(jnp.sum(lam_q2[l].astype(f32) * lam_k2[l].astype(f32))) + lam_init)

        qd, kd, vd, qs, ks, vs, ga, gb = project(hp, norm1_g[l], w_in[l], b_gate[l], qk_norm_q[l], qk_norm_k[l])

        def run_block(blk, kd=kd, vd=vd, ks=ks, vs=vs, lam=lam, lam_init=lam_init, sg=subln_g[l]):
            return attend(blk[0], blk[1], blk[2], kd, vd, ks, vs, pos_p, valid_p, rel_bias, lam, lam_init, sg)

        od_b, os_b = lax.map(run_block, (to_blocks(qd), to_blocks(qs), pos_p.reshape(n_blocks, Q_BLOCK)))
        hp = mix_and_mlp(hp, from_blocks(od_b), from_blocks(os_b), ga, gb, w_branch_a[l], w_branch_b[l],
                         w_o[l], norm2_g[l], w_up[l], w_down[l])
        kdp.append(kd[:, pad:])
        vdp.append(vd[:, pad:])
        ksp.append(ks[:, pad:])
        vsp.append(vs[:, pad:])

        qd2, kd2, vd2, qs2, ks2, vs2, ga2, gb2 = project(hs, norm1_g[l], w_in[l], b_gate[l], qk_norm_q[l], qk_norm_k[l])
        od2, os2 = attend(qd2, qs2, qpos_s,
                          gather_past(cache_k_diff[l], kd2), gather_past(cache_v_diff[l], vd2),
                          gather_past(cache_k_sb[l], ks2), gather_past(cache_v_sb[l], vs2),
                          kpos_s, valid_s, rel_bias, lam, lam_init, subln_g[l])
        hs = mix_and_mlp(hs, od2, os2, ga2, gb2, w_branch_a[l], w_branch_b[l],
                         w_o[l], norm2_g[l], w_up[l], w_down[l])
        kds.append(kd2)
        vds.append(vd2)
        kss.append(ks2)
        vss.append(vs2)

    y_prompt = hp[:, pad + N_META:]
    y_sample = hs
    new_k_diff_prompt = jnp.stack(kdp, axis=0)
    new_v_diff_prompt = jnp.stack(vdp, axis=0)
    new_k_sb_prompt = jnp.stack(ksp, axis=0)
    new_v_sb_prompt = jnp.stack(vsp, axis=0)
    new_k_diff_sample = jnp.stack(kds, axis=0)
    new_v_diff_sample = jnp.stack(vds, axis=0)
    new_k_sb_sample = jnp.stack(kss, axis=0)
    new_v_sb_sample = jnp.stack(vss, axis=0)
    return (y_prompt, y_sample, new_k_diff_prompt, new_v_diff_prompt, new_k_sb_prompt, new_v_sb_prompt,
            new_k_diff_sample, new_v_diff_sample, new_k_sb_sample, new_v_sb_sample)
```

```python
import functools
import math

import numpy as np
import jax
import jax.numpy as jnp
from jax import lax
from jax.experimental import pallas as pl
from jax.experimental.pallas import tpu as pltpu

f32 = jnp.float32
bf16 = jnp.bfloat16

D_MODEL = 2048
BATCH = 4
SEQ = 2048
DEC_BATCH = 32
DEC_SEQ = 8
PAST_LEN = 8192
PAGE_SIZE = 128
N_PAGES = PAST_LEN // PAGE_SIZE
N_META = 16
H_DIFF = 8
HD_DIFF = 64
W_DIFF = H_DIFF * 2 * HD_DIFF
H_SB = 16
HD_SB = 64
W_SB = H_SB * HD_SB
D_FF = 4 * D_MODEL
N_BUCKETS = 32
MAX_DISTANCE = 128
NORM_EPS = 1e-6
NEG_INF = -1e30
LAM_INIT = 0.8 - 0.6 * math.exp(-0.3 * 0)

ROWS_P = BATCH * SEQ
ROWS_S = DEC_BATCH * DEC_SEQ
SMALL = 512
ROWS = ROWS_P + SMALL
META_ROW0 = ROWS_P + ROWS_S
COL = 1024
QSCALE = HD_DIFF ** -0.5

VMEM_LIMIT = 56 * 1024 * 1024


def _bucket_thresholds():
    n = np.arange(0, 4 * MAX_DISTANCE)
    max_exact = N_BUCKETS // 2
    nf = np.maximum(n, 1).astype(np.float64)
    large = max_exact + (np.log(nf / max_exact) / math.log(MAX_DISTANCE / max_exact)
                         * (N_BUCKETS - max_exact)).astype(np.int64)
    bucket = np.where(n < max_exact, n, np.minimum(large, N_BUCKETS - 1))
    return [int(np.argmax(bucket >= b)) for b in range(N_BUCKETS)]


BUCKET_START = _bucket_thresholds()
FAR_DIST = BUCKET_START[-1]


def _params(sem):
    return pltpu.CompilerParams(dimension_semantics=sem, vmem_limit_bytes=VMEM_LIMIT)


TM_NORM = 512
NP_TILES = ROWS_P // TM_NORM


def _rms(x, g):
    ms = jnp.mean(x * x, axis=-1, keepdims=True)
    return x * lax.rsqrt(ms + NORM_EPS) * g


def _norm_kernel(xp_ref, xs_ref, g_ref, o_ref):
    i = pl.program_id(0)

    @pl.when(i < NP_TILES)
    def _():
        o_ref[...] = _rms(xp_ref[...], g_ref[...]).astype(bf16)

    @pl.when(i >= NP_TILES)
    def _():
        o_ref[...] = _rms(xs_ref[...], g_ref[...]).astype(bf16)


def _norm_call(xp, xs, g):
    return pl.pallas_call(
        _norm_kernel,
        grid=(ROWS // TM_NORM,),
        in_specs=[
            pl.BlockSpec((TM_NORM, D_MODEL), lambda i: (jnp.minimum(i, NP_TILES - 1), 0)),
            pl.BlockSpec((TM_NORM, D_MODEL), lambda i: (jnp.maximum(i - NP_TILES, 0), 0)),
            pl.BlockSpec((1, D_MODEL), lambda i: (0, 0)),
        ],
        out_specs=pl.BlockSpec((TM_NORM, D_MODEL), lambda i: (i, 0)),
        out_shape=jax.ShapeDtypeStruct((ROWS, D_MODEL), bf16),
        compiler_params=_params(("arbitrary",)),
        name="norm1",
    )(xp, xs, g)


TM_PROJ = 1088
assert ROWS % TM_PROJ == 0


def _group_sumsq(z, gmat):
    zz = z * z
    hi = zz.astype(bf16)
    lo = (zz - hi.astype(f32)).astype(bf16)
    parts = []
    for c in range(z.shape[1] // 256):
        sl = slice(256 * c, 256 * (c + 1))
        parts.append(jnp.dot(hi[:, sl], gmat, preferred_element_type=f32)
                     + jnp.dot(lo[:, sl], gmat, preferred_element_type=f32))
    return jnp.concatenate(parts, axis=1)


def _qk_norm(z, g, gmat):
    ss = _group_sumsq(z, gmat)
    return z * lax.rsqrt(ss * (1.0 / HD_DIFF) + NORM_EPS) * g


def _proj_q_kernel(x_ref, w_ref, g_ref, gmat_ref, o_ref):
    j = pl.program_id(0)
    z = jnp.dot(x_ref[...], w_ref[...], preferred_element_type=f32)

    @pl.when(j == 0)
    def _():
        o_ref[...] = (_qk_norm(z, g_ref[...], gmat_ref[...]) * QSCALE).astype(bf16)

    @pl.when(j == 1)
    def _():
        o_ref[...] = (z * QSCALE).astype(bf16)


def _proj_kv_kernel(x_ref, w_ref, g_ref, gmat_ref, o32_ref, o16_ref):
    j = pl.program_id(0)
    z = jnp.dot(x_ref[...], w_ref[...], preferred_element_type=f32)

    @pl.when(j == 0)
    def _():
        zn = _qk_norm(z, g_ref[...], gmat_ref[...])
        o32_ref[...] = zn
        o16_ref[...] = zn.astype(bf16)

    @pl.when(j > 0)
    def _():
        o32_ref[...] = z
        o16_ref[...] = z.astype(bf16)


def _proj_gate_kernel(x_ref, w_ref, b_ref, o_ref):
    z = jnp.dot(x_ref[...], w_ref[...], preferred_element_type=f32) + b_ref[...]
    o_ref[...] = (1.0 / (1.0 + jnp.exp(-z))).astype(bf16)


def _proj_calls(xn, w_in, gq, gk, gmat, b_gate):
    n_rt = ROWS // TM_PROJ
    x_spec = pl.BlockSpec((TM_PROJ, D_MODEL), lambda j, i: (i, 0))
    vec_spec = pl.BlockSpec((1, COL), lambda j, i: (0, 0))
    gmat_spec = pl.BlockSpec((256, 256), lambda j, i: (0, 0))
    out_spec = pl.BlockSpec((TM_PROJ, COL), lambda j, i: (i, j))
    sem = ("arbitrary", "arbitrary")

    q = pl.pallas_call(
        _proj_q_kernel,
        grid=(2, n_rt),
        in_specs=[x_spec, pl.BlockSpec((D_MODEL, COL), lambda j, i: (0, 3 * j)), vec_spec, gmat_spec],
        out_specs=out_spec,
        out_shape=jax.ShapeDtypeStruct((ROWS, 2 * COL), bf16),
        compiler_params=_params(sem),
        name="proj_q",
    )(xn, w_in, gq, gmat)

    kv32, kv16 = pl.pallas_call(
        _proj_kv_kernel,
        grid=(4, n_rt),
        in_specs=[x_spec, pl.BlockSpec((D_MODEL, COL), lambda j, i: (0, j + 1 + j // 2)), vec_spec, gmat_spec],
        out_specs=[out_spec, out_spec],
        out_shape=[jax.ShapeDtypeStruct((ROWS, 4 * COL), f32), jax.ShapeDtypeStruct((ROWS, 4 * COL), bf16)],
        compiler_params=_params(sem),
        name="proj_kv",
    )(xn, w_in, gk, gmat)

    gates = pl.pallas_call(
        _proj_gate_kernel,
        grid=(4, n_rt),
        in_specs=[x_spec, pl.BlockSpec((D_MODEL, COL), lambda j, i: (0, j + 6)),
                  pl.BlockSpec((1, COL), lambda j, i: (0, j))],
        out_specs=out_spec,
        out_shape=jax.ShapeDtypeStruct((ROWS, 4 * COL), bf16),
        compiler_params=_params(sem),
        name="proj_gate",
    )(xn, w_in, b_gate)
    return q, kv32, kv16, gates


def _nt_dot(a, b):
    return lax.dot_general(a, b, (((1,), (1,)), ((), ())), preferred_element_type=f32)


def _diff_update(s, pv_fn, m_ref, l_ref, acc_ref):
    m_old = m_ref[...]
    m_new = jnp.maximum(m_old, jnp.max(s, axis=1, keepdims=True))
    alpha = jnp.exp(m_old - m_new)
    p = jnp.exp(s - m_new)
    l_ref[...] = alpha * l_ref[...] + jnp.sum(p, axis=1, keepdims=True)
    acc_ref[...] = alpha * acc_ref[...] + pv_fn(p.astype(bf16))
    m_ref[...] = m_new


def _sb_update(z, mask, pv_fn, u_ref, c_ref, acc_ref):
    sp = jnp.maximum(z, 0.0) + jnp.log1p(jnp.exp(-jnp.abs(z)))
    l1m = -sp
    if mask is not None:
        l1m = jnp.where(mask, l1m, 0.0)
    hi = l1m.astype(bf16)
    lo = (l1m - hi.astype(f32)).astype(bf16)
    u = u_ref[...]
    later = jnp.dot(hi, u, preferred_element_type=f32) + jnp.dot(lo, u, preferred_element_type=f32)
    c_old = c_ref[...]
    w = jnp.exp((z - sp) + (later + c_old))
    if mask is not None:
        w = jnp.where(mask, w, 0.0)
    acc_ref[...] += pv_fn(w.astype(bf16))
    c_ref[...] = c_old + jnp.sum(l1m, axis=1, keepdims=True)


def _bias_chain(dist, rb_of):
    b = jnp.where(dist >= BUCKET_START[1], rb_of(1), rb_of(0))
    for k in range(2, N_BUCKETS):
        b = jnp.where(dist >= BUCKET_START[k], rb_of(k), b)
    return b


def _lam(lq1, lk1, lq2, lk2):
    return (jnp.exp(jnp.sum(lq1 * lk1, axis=1, keepdims=True))
            - jnp.exp(jnp.sum(lq2 * lk2, axis=1, keepdims=True)) + LAM_INIT)


def _sub_norm(od, sg):
    ms = jnp.mean(od * od, axis=-1, keepdims=True)
    return (od * lax.rsqrt(ms + NORM_EPS) * sg) * (1.0 - LAM_INIT)


def _nn(v):
    return lambda p: jnp.dot(p, v, preferred_element_type=f32)


TQ = 256
NQ = SEQ // TQ
META_TILE = 128
assert TQ + 1 >= FAR_DIST and N_META + TQ - (N_META - 1) >= FAR_DIST


def _attn_kernel(rb_ref, qd_ref, qs_ref, kd_ref, vd_ref, ks_ref, vs_ref,
                 kdm_ref, vdm_ref, ksm_ref, vsm_ref,
                 lq1_ref, lk1_ref, lq2_ref, lk2_ref, sg_ref, u_ref, um_ref,
                 od_ref, os_ref,
                 b0_ref, b1_ref, bm_ref, md_ref, ld_ref, accd_ref, cs_ref, accs_ref):
    h = pl.program_id(0)
    qi = pl.program_id(2)
    R = 2 * TQ

    def rb_of(k):
        return rb_ref[k, h]

    row = lax.broadcasted_iota(jnp.int32, (R, TQ), 0) % TQ
    col = lax.broadcasted_iota(jnp.int32, (R, TQ), 1)
    mrow = lax.broadcasted_iota(jnp.int32, (R, META_TILE), 0) % TQ
    mcol = lax.broadcasted_iota(jnp.int32, (R, META_TILE), 1)

    @pl.when(qi == 0)
    def _():
        b0_ref[...] = _bias_chain(row - col, rb_of)
        b1_ref[...] = _bias_chain(TQ + row - col, rb_of)
        bm_ref[...] = _bias_chain(N_META + mrow - mcol, rb_of)

    @pl.when(qi == 1)
    def _():
        bm_ref[...] = jnp.full((R, META_TILE), rb_of(N_BUCKETS - 1), f32)

    md_ref[...] = jnp.full((R, 1), NEG_INF, f32)
    ld_ref[...] = jnp.zeros((R, 1), f32)
    accd_ref[...] = jnp.zeros((R, 128), f32)
    cs_ref[...] = jnp.zeros((R, 1), f32)
    accs_ref[...] = jnp.zeros((R, 128), f32)

    lane = lax.broadcasted_iota(jnp.int32, (1, 128), 1)
    lo_half = (lane < HD_DIFF).astype(f32).astype(bf16)
    hi_half = (lane >= HD_DIFF).astype(f32).astype(bf16)
    qd = qd_ref[...]
    qs = qs_ref[...]
    qd2 = jnp.concatenate([qd * lo_half, qd * hi_half], axis=0)
    qs2 = jnp.concatenate([qs * lo_half, qs * hi_half], axis=0)

    def tile(j, bias, dmask, smask):
        r0 = pl.multiple_of(j * TQ, TQ)
        sl = pl.ds(r0, TQ)
        s = _nt_dot(qd2, kd_ref[sl, :]) + bias
        if dmask is not None:
            s = jnp.where(dmask, s, NEG_INF)
        _diff_update(s, _nn(vd_ref[sl, :]), md_ref, ld_ref, accd_ref)
        _sb_update(_nt_dot(qs2, ks_ref[sl, :]), smask, _nn(vs_ref[sl, :]), u_ref, cs_ref, accs_ref)

    tile(qi, b0_ref[...], row >= col, row > col)

    @pl.when(qi >= 1)
    def _():
        tile(qi - 1, b1_ref[...], None, None)

    def far(t, carry):
        tile(qi - 2 - t, rb_of(N_BUCKETS - 1), None, None)
        return carry

    lax.fori_loop(0, qi - 1, far, 0)

    mvalid = mcol < N_META
    sm = jnp.where(mvalid, _nt_dot(qd2, kdm_ref[...]) + bm_ref[...], NEG_INF)
    _diff_update(sm, _nn(vdm_ref[...]), md_ref, ld_ref, accd_ref)
    _sb_update(_nt_dot(qs2, ksm_ref[...]), mvalid, _nn(vsm_ref[...]), um_ref, cs_ref, accs_ref)

    lam = _lam(lq1_ref[...], lk1_ref[...], lq2_ref[...], lk2_ref[...])
    o = accd_ref[...] / ld_ref[...]
    od = o[:TQ] - lam * o[TQ:]
    od_ref[...] = _sub_norm(od, sg_ref[...]).astype(bf16)
    accs = accs_ref[...]
    os_ref[...] = jnp.where(lane < HD_SB, accs[:TQ], accs[TQ:]).astype(bf16)


def _attn_call(rel_bias, q, kv16, lq1, lk1, lq2, lk2, sg, u, um):
    R = 2 * TQ
    nh = H_DIFF
    meta_blk = META_ROW0 // META_TILE
    assert META_ROW0 % META_TILE == 0

    def qspec(off):
        return pl.BlockSpec((TQ, 128), lambda h, b, i: (b * NQ + i, off + h))

    def kvspec(off):
        return pl.BlockSpec((SEQ, 128), lambda h, b, i: (b, off + h))

    def mspec(off):
        return pl.BlockSpec((META_TILE, 128), lambda h, b, i: (meta_blk, off + h))

    def const(shape):
        return pl.BlockSpec(shape, lambda h, b, i: (0, 0))

    out_spec = pl.BlockSpec((TQ, 128), lambda h, b, i: (b * NQ + i, h))
    return pl.pallas_call(
        _attn_kernel,
        grid=(nh, BATCH, NQ),
        in_specs=[pl.BlockSpec(memory_space=pltpu.SMEM),
                  qspec(0), qspec(nh),
                  kvspec(0), kvspec(nh), kvspec(2 * nh), kvspec(3 * nh),
                  mspec(0), mspec(nh), mspec(2 * nh), mspec(3 * nh),
                  const((1, HD_DIFF)), const((1, HD_DIFF)), const((1, HD_DIFF)), const((1, HD_DIFF)),
                  const((1, 128)), const((TQ, TQ)), const((META_TILE, META_TILE))],
        out_specs=[out_spec, out_spec],
        out_shape=[jax.ShapeDtypeStruct((ROWS_P, W_DIFF), bf16), jax.ShapeDtypeStruct((ROWS_P, W_SB), bf16)],
        scratch_shapes=[pltpu.VMEM((R, TQ), f32), pltpu.VMEM((R, TQ), f32), pltpu.VMEM((R, META_TILE), f32),
                        pltpu.VMEM((R, 1), f32), pltpu.VMEM((R, 1), f32), pltpu.VMEM((R, 128), f32),
                        pltpu.VMEM((R, 1), f32), pltpu.VMEM((R, 128), f32)],
        compiler_params=_params(("arbitrary", "arbitrary", "arbitrary")),
        name="attn_prompt",
    )(rel_bias, q, q, kv16, kv16, kv16, kv16, kv16, kv16, kv16, kv16, lq1, lk1, lq2, lk2, sg, u, um)


PPS = 4
NSTEP = N_PAGES // PPS
RS = 128
PH = PAGE_SIZE * H_DIFF
assert H_DIFF * 2 * DEC_SEQ == RS and H_SB * DEC_SEQ == RS
assert PAGE_SIZE + 1 >= FAR_DIST


def _diag_blocks_sb(pv):
    return jnp.concatenate([pv[8 * h:8 * (h + 1), 128 * (h // 2):128 * (h // 2 + 1)] for h in range(H_SB)], axis=0)


def _sample_kernel(pt_ref, rbrow_ref, qx_ref, qs_ref, kdn_ref, vdn_ref, ksn_ref, vsn_ref, *rest):
    cache_refs = rest[:4 * PPS]
    (lq1_ref, lk1_ref, lq2_ref, lk2_ref, sg_ref, u_ref, od_ref, os_ref,
     alast_ref, afar_ref, md_ref, ld_ref, accd_ref, cs_ref, accs_ref) = rest[4 * PPS:]
    s = pl.program_id(1)
    qx = qx_ref[...]
    qs = qs_ref[...]

    def rb_col(k):
        return rbrow_ref[:, k:k + 1]

    def diff_geometry(n_lanes, base):
        r = lax.broadcasted_iota(jnp.int32, (RS, n_lanes), 0)
        ln = lax.broadcasted_iota(jnp.int32, (RS, n_lanes), 1)
        same_head = (ln % H_DIFF) == (r // (2 * DEC_SEQ))
        dist = base + (r % DEC_SEQ) - (ln // H_DIFF)
        return same_head, dist

    def diff_page(kd, vd, add):
        sc = _nt_dot(qx, kd.astype(bf16)) + add
        _diff_update(sc, _nn(vd.astype(bf16)), md_ref, ld_ref, accd_ref)

    @pl.when(s == 0)
    def _():
        md_ref[...] = jnp.full((RS, 1), NEG_INF, f32)
        ld_ref[...] = jnp.zeros((RS, 1), f32)
        accd_ref[...] = jnp.zeros((RS, 128), f32)
        cs_ref[...] = jnp.zeros((RS, 1), f32)
        accs_ref[...] = jnp.zeros((RS, 128), f32)
        same, dist = diff_geometry(DEC_SEQ * H_DIFF, 0)
        add = jnp.where(same & (dist >= 0), _bias_chain(dist, rb_col), NEG_INF)
        diff_page(kdn_ref[...], vdn_ref[...], add)
        zpad = jnp.zeros((PAGE_SIZE - DEC_SEQ, COL), f32)
        ksn = jnp.concatenate([ksn_ref[...], zpad], axis=0).astype(bf16)
        vsn = jnp.concatenate([vsn_ref[...], zpad], axis=0).astype(bf16)
        qidx = lax.broadcasted_iota(jnp.int32, (RS, PAGE_SIZE), 0) % DEC_SEQ
        key = lax.broadcasted_iota(jnp.int32, (RS, PAGE_SIZE), 1)
        smask = (qidx > key) & (key < DEC_SEQ)
        _sb_update(_nt_dot(qs, ksn), smask, lambda w: _diag_blocks_sb(jnp.dot(w, vsn, preferred_element_type=f32)),
                   u_ref, cs_ref, accs_ref)
        same, dist = diff_geometry(PH, PAGE_SIZE)
        alast_ref[...] = jnp.where(same, _bias_chain(dist, rb_col), NEG_INF)
        afar_ref[...] = jnp.where(same, rb_col(N_BUCKETS - 1), NEG_INF)

    @pl.when(s == 1)
    def _():
        alast_ref[...] = afar_ref[...]

    for k in range(PPS):
        kd_ref, vd_ref, ks_ref, vs_ref = cache_refs[4 * k:4 * k + 4]
        diff_page(kd_ref[...], vd_ref[...], alast_ref[...] if k == 0 else afar_ref[...])
        vt = vs_ref[...].astype(bf16)
        _sb_update(jnp.dot(qs, ks_ref[...].astype(bf16), preferred_element_type=f32), None,
                   lambda w, vt=vt: _diag_blocks_sb(_nt_dot(w, vt)), u_ref, cs_ref, accs_ref)

    @pl.when(s == NSTEP - 1)
    def _():
        lam = _lam(lq1_ref[...], lk1_ref[...], lq2_ref[...], lk2_ref[...])
        o = accd_ref[...] / ld_ref[...]
        sg = sg_ref[...]
        heads = []
        for h in range(H_DIFF):
            od = o[16 * h:16 * h + 8] - lam * o[16 * h + 8:16 * h + 16]
            heads.append(_sub_norm(od, sg))
        od_ref[...] = jnp.concatenate(heads, axis=1).astype(bf16)
        accs = accs_ref[...]
        lane = lax.broadcasted_iota(jnp.int32, (1, 128), 1)
        pairs = [jnp.where(lane < HD_SB, accs[16 * t:16 * t + 8], accs[16 * t + 8:16 * t + 16])
                 for t in range(H_SB // 2)]
        os_ref[...] = jnp.concatenate(pairs, axis=1).astype(bf16)


def _sample_call(page_table, rbrow, qx, qbd_s, kdn, vdn, ksn, vsn, caches, lq1, lk1, lq2, lk2, sg, u):
    def const2(shape):
        return pl.BlockSpec(shape, lambda b, s, pt: (0, 0))

    def per_b(shape):
        return pl.BlockSpec((None,) + shape, lambda b, s, pt: (b, 0, 0))

    def cache_spec(k):
        return pl.BlockSpec((None, COL, PAGE_SIZE),
                            lambda b, s, pt: (pt[b, N_PAGES - 1 - (s * PPS + k)], 0, 0))

    in_specs = [const2((RS, N_BUCKETS)), per_b((RS, 128)), per_b((RS, COL)),
                per_b((DEC_SEQ * H_DIFF, 128)), per_b((DEC_SEQ * H_DIFF, 128)),
                per_b((DEC_SEQ, COL)), per_b((DEC_SEQ, COL))]
    args = [rbrow, qx, qbd_s, kdn, vdn, ksn, vsn]
    for k in range(PPS):
        for c in caches:
            in_specs.append(cache_spec(k))
            args.append(c)
    in_specs += [const2((1, HD_DIFF))] * 4 + [const2((1, 128)), const2((PAGE_SIZE, PAGE_SIZE))]
    args += [lq1, lk1, lq2, lk2, sg, u]
    out_spec = per_b((DEC_SEQ, COL))
    grid_spec = pltpu.PrefetchScalarGridSpec(
        num_scalar_prefetch=1,
        grid=(DEC_BATCH, NSTEP),
        in_specs=in_specs,
        out_specs=[out_spec, out_spec],
        scratch_shapes=[pltpu.VMEM((RS, PH), f32), pltpu.VMEM((RS, PH), f32),
                        pltpu.VMEM((RS, 1), f32), pltpu.VMEM((RS, 1), f32), pltpu.VMEM((RS, 128), f32),
                        pltpu.VMEM((RS, 1), f32), pltpu.VMEM((RS, 128), f32)],
    )
    return pl.pallas_call(
        _sample_kernel,
        grid_spec=grid_spec,
        out_shape=[jax.ShapeDtypeStruct((DEC_BATCH, DEC_SEQ, W_DIFF), bf16),
                   jax.ShapeDtypeStruct((DEC_BATCH, DEC_SEQ, W_SB), bf16)],
        compiler_params=_params(("arbitrary", "arbitrary")),
        name="attn_sample",
    )(page_table, *args)


TM_MIX = 256
NP_MIX = ROWS_P // TM_MIX


def _mix_kernel(odp_ref, osp_ref, ods_ref, oss_ref, g_ref, xp_ref, xs_ref,
                wa_ref, wb_ref, wo_ref, n2_ref, h2_ref, xn2_ref):
    i = pl.program_id(0)

    def body(od, osb, x):
        ya = jnp.dot(od, wa_ref[...], preferred_element_type=f32)
        yb = jnp.dot(osb, wb_ref[...], preferred_element_type=f32)
        g = g_ref[...]
        mix = g[:, :D_MODEL].astype(f32) * ya + g[:, D_MODEL:].astype(f32) * yb
        h2 = x + jnp.dot(mix.astype(bf16), wo_ref[...], preferred_element_type=f32)
        h2_ref[...] = h2
        xn2_ref[...] = _rms(h2, n2_ref[...]).astype(bf16)

    @pl.when(i < NP_MIX)
    def _():
        body(odp_ref[...], osp_ref[...], xp_ref[...])

    @pl.when(i >= NP_MIX)
    def _():
        body(ods_ref[...], oss_ref[...], xs_ref[...])


def _mix_call(od_p, os_p, od_s, os_s, gates, xp, xs, wa, wb, wo, n2):
    def prow(i):
        return (jnp.minimum(i, NP_MIX - 1), 0)

    def srow(i):
        return (jnp.maximum(i - NP_MIX, 0), 0)

    def full(shape):
        return pl.BlockSpec(shape, lambda i: (0, 0), pipeline_mode=pl.Buffered(1))

    return pl.pallas_call(
        _mix_kernel,
        grid=(ROWS // TM_MIX,),
        in_specs=[pl.BlockSpec((TM_MIX, W_DIFF), prow), pl.BlockSpec((TM_MIX, W_SB), prow),
                  pl.BlockSpec((TM_MIX, W_DIFF), srow), pl.BlockSpec((TM_MIX, W_SB), srow),
                  pl.BlockSpec((TM_MIX, 2 * D_MODEL), lambda i: (i, 0)),
                  pl.BlockSpec((TM_MIX, D_MODEL), prow), pl.BlockSpec((TM_MIX, D_MODEL), srow),
                  full((W_DIFF, D_MODEL)), full((W_SB, D_MODEL)), full((D_MODEL, D_MODEL)),
                  full((1, D_MODEL))],
        out_specs=[pl.BlockSpec((TM_MIX, D_MODEL), lambda i: (i, 0)),
                   pl.BlockSpec((TM_MIX, D_MODEL), lambda i: (i, 0))],
        out_shape=[jax.ShapeDtypeStruct((ROWS, D_MODEL), f32), jax.ShapeDtypeStruct((ROWS, D_MODEL), bf16)],
        compiler_params=_params(("arbitrary",)),
        name="mix_out",
    )(od_p, os_p, od_s, os_s, gates, xp, xs, wa, wb, wo, n2)


TM_MLP = 512
TF_MLP = 1024


def _mlp_kernel(xn_ref, h2_ref, wu_ref, wd_ref, o_ref):
    f = pl.program_id(1)
    u = jnp.maximum(jnp.dot(xn_ref[...], wu_ref[...], preferred_element_type=f32), 0.0)
    part = jnp.dot((u * u).astype(bf16), wd_ref[...], preferred_element_type=f32)

    @pl.when(f == 0)
    def _():
        o_ref[...] = h2_ref[...] + part

    @pl.when(f > 0)
    def _():
        o_ref[...] += part


def _mlp_call(xn2, h2, wu, wd):
    return pl.pallas_call(
        _mlp_kernel,
        grid=(ROWS // TM_MLP, D_FF // TF_MLP),
        in_specs=[pl.BlockSpec((TM_MLP, D_MODEL), lambda i, f: (i, 0)),
                  pl.BlockSpec((TM_MLP, D_MODEL), lambda i, f: (i, 0)),
                  pl.BlockSpec((D_MODEL, TF_MLP), lambda i, f: (0, f)),
                  pl.BlockSpec((TF_MLP, D_MODEL), lambda i, f: (f, 0))],
        out_specs=pl.BlockSpec((TM_MLP, D_MODEL), lambda i, f: (i, 0)),
        out_shape=jax.ShapeDtypeStruct((ROWS, D_MODEL), f32),
        compiler_params=_params(("arbitrary", "arbitrary")),
        name="mlp",
    )(xn2, h2, wu, wd)


def _later_matrix(n):
    r = np.arange(n)
    return jnp.asarray((r[:, None] > r[None, :]).astype(np.float32), dtype=bf16)


def _block_diag_queries(qrows, n_groups):
    gw = COL // n_groups
    grp = jnp.arange(COL, dtype=jnp.int32) // gw
    keep = grp[None, :] == jnp.arange(n_groups, dtype=jnp.int32)[:, None]
    out = jnp.where(keep[None, :, None, :], qrows[:, None, :, :], jnp.zeros((), qrows.dtype))
    return out.reshape(DEC_BATCH, n_groups * DEC_SEQ, COL)


def kernel(x_prompt, x_sample, cache_k_diff, cache_v_diff, cache_k_sb, cache_v_sb, page_table, meta_tokens,
           rel_bias, norm1_g, w_in, b_gate, qk_norm_q, qk_norm_k, lam_q1, lam_k1, lam_q2, lam_k2, subln_g,
           w_branch_a, w_branch_b, w_o, norm2_g, w_up, w_down):
    l = 0
    xp = x_prompt.reshape(ROWS_P, D_MODEL)
    xs = jnp.concatenate([x_sample.reshape(ROWS_S, D_MODEL), meta_tokens.astype(f32),
                          jnp.zeros((SMALL - ROWS_S - N_META, D_MODEL), f32)], axis=0)
    w_in16 = w_in[l].astype(bf16)
    wa16 = w_branch_a[l].astype(bf16)
    wb16 = w_branch_b[l].astype(bf16)
    wo16 = w_o[l].astype(bf16)
    wu16 = w_up[l].astype(bf16)
    wd16 = w_down[l].astype(bf16)
    gq = jnp.tile(qk_norm_q[l].astype(f32), COL // HD_DIFF)[None]
    gk = jnp.tile(qk_norm_k[l].astype(f32), COL // HD_DIFF)[None]
    grp = np.arange(256) // HD_DIFF
    gmat = jnp.asarray((grp[:, None] == grp[None, :]).astype(np.float32), dtype=bf16)
    lq1, lk1, lq2, lk2 = (a[l].astype(f32)[None] for a in (lam_q1, lam_k1, lam_q2, lam_k2))
    sg = subln_g[l].astype(f32)[None]
    rb = rel_bias.astype(f32)

    xn = _norm_call(xp, xs, norm1_g[l].astype(f32)[None])
    q, kv32, kv16, gates = _proj_calls(xn, w_in16, gq, gk, gmat, b_gate[l].astype(f32)[None])

    od_p, os_p = _attn_call(rb, q, kv16, lq1, lk1, lq2, lk2, sg, _later_matrix(TQ), _later_matrix(META_TILE))

    q_s = q[ROWS_P:ROWS_P + ROWS_S].reshape(DEC_BATCH, DEC_SEQ, 2 * COL)
    qd_s = jnp.transpose(q_s[..., :COL].reshape(DEC_BATCH, DEC_SEQ, H_DIFF, 2 * HD_DIFF), (0, 2, 1, 3))
    half = (jnp.arange(2 * HD_DIFF, dtype=jnp.int32) // HD_DIFF)[None, :] == jnp.arange(2, dtype=jnp.int32)[:, None]
    qx = jnp.where(half[None, None, :, None, :], qd_s[:, :, None], jnp.zeros((), bf16)).reshape(DEC_BATCH, RS, 128)
    qbd_s = _block_diag_queries(q_s[..., COL:], H_SB)
    new32 = kv32[ROWS_P:ROWS_P + ROWS_S].reshape(DEC_BATCH, DEC_SEQ, 4 * COL)
    kdn = new32[..., 0:COL].reshape(DEC_BATCH, DEC_SEQ * H_DIFF, 2 * HD_DIFF)
    vdn = new32[..., COL:2 * COL].reshape(DEC_BATCH, DEC_SEQ * H_DIFF, 2 * HD_DIFF)
    ksn = new32[..., 2 * COL:3 * COL]
    vsn = new32[..., 3 * COL:4 * COL]
    n_phys = cache_k_diff.shape[1]
    caches = [cache_k_diff[l].reshape(n_phys, COL, 2 * HD_DIFF), cache_v_diff[l].reshape(n_phys, COL, 2 * HD_DIFF),
              jnp.transpose(cache_k_sb[l], (0, 2, 3, 1)).reshape(n_phys, COL, PAGE_SIZE),
              jnp.transpose(cache_v_sb[l], (0, 2, 3, 1)).reshape(n_phys, COL, PAGE_SIZE)]
    rbrow = jnp.repeat(rb.T, 2 * DEC_SEQ, axis=0)
    od_s, os_s = _sample_call(page_table, rbrow, qx, qbd_s, kdn, vdn, ksn, vsn, caches, lq1, lk1, lq2, lk2, sg,
                              _later_matrix(PAGE_SIZE))
    pad = jnp.zeros((SMALL - ROWS_S, COL), bf16)
    od_s = jnp.concatenate([od_s.reshape(ROWS_S, COL), pad], axis=0)
    os_s = jnp.concatenate([os_s.reshape(ROWS_S, COL), pad], axis=0)

    h2, xn2 = _mix_call(od_p, os_p, od_s, os_s, gates, xp, xs, wa16, wb16, wo16, norm2_g[l].astype(f32)[None])
    y = _mlp_call(xn2, h2, wu16, wd16)

    y_prompt = y[:ROWS_P].reshape(BATCH, SEQ, D_MODEL)
    y_sample = y[ROWS_P:ROWS_P + ROWS_S].reshape(DEC_BATCH, DEC_SEQ, D_MODEL)

    def prompt_kv(c, heads, width):
        body = kv32[:ROWS_P, c * COL:(c + 1) * COL].reshape(BATCH, SEQ, heads, width)
        meta = kv32[META_ROW0:META_ROW0 + N_META, c * COL:(c + 1) * COL].reshape(1, N_META, heads, width)
        return jnp.concatenate([jnp.broadcast_to(meta, (BATCH, N_META, heads, width)), body], axis=1)[None]

    def sample_kv(c, heads, width):
        return kv32[ROWS_P:ROWS_P + ROWS_S, c * COL:(c + 1) * COL].reshape(1, DEC_BATCH, DEC_SEQ, heads, width)

    return (y_prompt, y_sample,
            prompt_kv(0, H_DIFF, 2 * HD_DIFF), prompt_kv(1, H_DIFF, 2 * HD_DIFF),
            prompt_kv(2, H_SB, HD_SB), prompt_kv(3, H_SB, HD_SB),
            sample_kv(0, H_DIFF, 2 * HD_DIFF), sample_kv(1, H_DIFF, 2 * HD_DIFF),
            sample_kv(2, H_SB, HD_SB), sample_kv(3, H_SB, HD_SB))
```

```python
import functools
import math

import numpy as np
import jax
import jax.numpy as jnp
from jax import lax
from jax.experimental import pallas as pl
from jax.experimental.pallas import tpu as pltpu

f32 = jnp.float32
bf16 = jnp.bfloat16

D_MODEL = 2048
BATCH = 4
SEQ = 2048
DEC_BATCH = 32
DEC_SEQ = 8
PAST_LEN = 8192
PAGE_SIZE = 128
N_PAGES = PAST_LEN // PAGE_SIZE
N_META = 16
H_DIFF = 8
HD_DIFF = 64
W_DIFF = H_DIFF * 2 * HD_DIFF
H_SB = 16
HD_SB = 64
W_SB = H_SB * HD_SB
D_FF = 4 * D_MODEL
N_BUCKETS = 32
MAX_DISTANCE = 128
NORM_EPS = 1e-6
NEG_INF = -1e30
LAM_INIT = 0.8 - 0.6 * math.exp(-0.3 * 0)

ROWS_P = BATCH * SEQ
ROWS_S = DEC_BATCH * DEC_SEQ
SMALL = 512
ROWS = ROWS_P + SMALL
META_ROW0 = ROWS_P + ROWS_S
COL = 1024
QSCALE = HD_DIFF ** -0.5

VMEM_LIMIT = 56 * 1024 * 1024


def _bucket_thresholds():
    n = np.arange(0, 4 * MAX_DISTANCE)
    max_exact = N_BUCKETS // 2
    nf = np.maximum(n, 1).astype(np.float64)
    large = max_exact + (np.log(nf / max_exact) / math.log(MAX_DISTANCE / max_exact)
                         * (N_BUCKETS - max_exact)).astype(np.int64)
    bucket = np.where(n < max_exact, n, np.minimum(large, N_BUCKETS - 1))
    return [int(np.argmax(bucket >= b)) for b in range(N_BUCKETS)]


BUCKET_START = _bucket_thresholds()
FAR_DIST = BUCKET_START[-1]


def _params(sem):
    return pltpu.CompilerParams(dimension_semantics=sem, vmem_limit_bytes=VMEM_LIMIT)


TM_NORM = 512
NP_TILES = ROWS_P // TM_NORM


def _rms(x, g):
    ms = jnp.mean(x * x, axis=-1, keepdims=True)
    return x * lax.rsqrt(ms + NORM_EPS) * g


def _norm_kernel(xp_ref, xs_ref, g_ref, o_ref):
    i = pl.program_id(0)

    @pl.when(i < NP_TILES)
    def _():
        o_ref[...] = _rms(xp_ref[...], g_ref[...]).astype(bf16)

    @pl.when(i >= NP_TILES)
    def _():
        o_ref[...] = _rms(xs_ref[...], g_ref[...]).astype(bf16)


def _norm_call(xp, xs, g):
    return pl.pallas_call(
        _norm_kernel,
        grid=(ROWS // TM_NORM,),
        in_specs=[
            pl.BlockSpec((TM_NORM, D_MODEL), lambda i: (jnp.minimum(i, NP_TILES - 1), 0)),
            pl.BlockSpec((TM_NORM, D_MODEL), lambda i: (jnp.maximum(i - NP_TILES, 0), 0)),
            pl.BlockSpec((1, D_MODEL), lambda i: (0, 0)),
        ],
        out_specs=pl.BlockSpec((TM_NORM, D_MODEL), lambda i: (i, 0)),
        out_shape=jax.ShapeDtypeStruct((ROWS, D_MODEL), bf16),
        compiler_params=_params(("arbitrary",)),
        name="norm1",
    )(xp, xs, g)


TM_PROJ = 1088
assert ROWS % TM_PROJ == 0


def _group_sumsq(z, gmat):
    zz = z * z
    hi = zz.astype(bf16)
    lo = (zz - hi.astype(f32)).astype(bf16)
    parts = []
    for c in range(z.shape[1] // 256):
        sl = slice(256 * c, 256 * (c + 1))
        parts.append(jnp.dot(hi[:, sl], gmat, preferred_element_type=f32)
                     + jnp.dot(lo[:, sl], gmat, preferred_element_type=f32))
    return jnp.concatenate(parts, axis=1)


def _qk_norm(z, g, gmat):
    ss = _group_sumsq(z, gmat)
    return z * lax.rsqrt(ss * (1.0 / HD_DIFF) + NORM_EPS) * g


def _proj_q_kernel(x_ref, w_ref, g_ref, gmat_ref, o_ref):
    j = pl.program_id(0)
    z = jnp.dot(x_ref[...], w_ref[...], preferred_element_type=f32)

    @pl.when(j == 0)
    def _():
        o_ref[...] = (_qk_norm(z, g_ref[...], gmat_ref[...]) * QSCALE).astype(bf16)

    @pl.when(j == 1)
    def _():
        o_ref[...] = (z * QSCALE).astype(bf16)


def _proj_kv_kernel(x_ref, w_ref, g_ref, gmat_ref, o32_ref, o16_ref):
    j = pl.program_id(0)
    z = jnp.dot(x_ref[...], w_ref[...], preferred_element_type=f32)

    @pl.when(j == 0)
    def _():
        zn = _qk_norm(z, g_ref[...], gmat_ref[...])
        o32_ref[...] = zn
        o16_ref[...] = zn.astype(bf16)

    @pl.when(j > 0)
    def _():
        o32_ref[...] = z
        o16_ref[...] = z.astype(bf16)


def _proj_gate_kernel(x_ref, w_ref, b_ref, o_ref):
    z = jnp.dot(x_ref[...], w_ref[...], preferred_element_type=f32) + b_ref[...]
    o_ref[...] = (1.0 / (1.0 + jnp.exp(-z))).astype(bf16)


def _proj_calls(xn, w_in, gq, gk, gmat, b_gate):
    n_rt = ROWS // TM_PROJ
    x_spec = pl.BlockSpec((TM_PROJ, D_MODEL), lambda j, i: (i, 0))
    vec_spec = pl.BlockSpec((1, COL), lambda j, i: (0, 0))
    gmat_spec = pl.BlockSpec((256, 256), lambda j, i: (0, 0))
    out_spec = pl.BlockSpec((TM_PROJ, COL), lambda j, i: (i, j))
    sem = ("arbitrary", "arbitrary")

    q = pl.pallas_call(
        _proj_q_kernel,
        grid=(2, n_rt),
        in_specs=[x_spec, pl.BlockSpec((D_MODEL, COL), lambda j, i: (0, 3 * j)), vec_spec, gmat_spec],
        out_specs=out_spec,
        out_shape=jax.ShapeDtypeStruct((ROWS, 2 * COL), bf16),
        compiler_params=_params(sem),
        name="proj_q",
    )(xn, w_in, gq, gmat)

    kv32, kv16 = pl.pallas_call(
        _proj_kv_kernel,
        grid=(4, n_rt),
        in_specs=[x_spec, pl.BlockSpec((D_MODEL, COL), lambda j, i: (0, j + 1 + j // 2)), vec_spec, gmat_spec],
        out_specs=[out_spec, out_spec],
        out_shape=[jax.ShapeDtypeStruct((ROWS, 4 * COL), f32), jax.ShapeDtypeStruct((ROWS, 4 * COL), bf16)],
        compiler_params=_params(sem),
        name="proj_kv",
    )(xn, w_in, gk, gmat)

    gates = pl.pallas_call(
        _proj_gate_kernel,
        grid=(4, n_rt),
        in_specs=[x_spec, pl.BlockSpec((D_MODEL, COL), lambda j, i: (0, j + 6)),
                  pl.BlockSpec((1, COL), lambda j, i: (0, j))],
        out_specs=out_spec,
        out_shape=jax.ShapeDtypeStruct((ROWS, 4 * COL), bf16),
        compiler_params=_params(sem),
        name="proj_gate",
    )(xn, w_in, b_gate)
    return q, kv32, kv16, gates


def _nt_dot(a, b):
    return lax.dot_general(a, b, (((1,), (1,)), ((), ())), preferred_element_type=f32)


def _diff_update(parts, m_ref, l_ref, acc_ref):
    m_old = m_ref[...]
    m_new = m_old
    for s, _ in parts:
        m_new = jnp.maximum(m_new, jnp.max(s, axis=1, keepdims=True))
    alpha = jnp.exp(m_old - m_new)
    l_new = alpha * l_ref[...]
    acc = alpha * acc_ref[...]
    for s, pv_fn in parts:
        p = jnp.exp(s - m_new)
        l_new = l_new + jnp.sum(p, axis=1, keepdims=True)
        acc = acc + pv_fn(p.astype(bf16))
    l_ref[...] = l_new
    acc_ref[...] = acc
    m_ref[...] = m_new


def _sb_update(parts, u_ref, c_ref, acc_ref):
    u = u_ref[...]
    pre = []
    for z, mask, _ in parts:
        sp = jnp.maximum(z, 0.0) + jnp.log(1.0 + jnp.exp(-jnp.abs(z)))
        l1m = -sp
        if mask is not None:
            l1m = jnp.where(mask, l1m, 0.0)
        hi = l1m.astype(bf16)
        lo = (l1m - hi.astype(f32)).astype(bf16)
        later = jnp.dot(hi, u, preferred_element_type=f32) + jnp.dot(lo, u, preferred_element_type=f32)
        pre.append(((z - sp) + later, jnp.sum(l1m, axis=1, keepdims=True)))
    c = c_ref[...]
    acc = acc_ref[...]
    for (base, tot), (_, mask, pv_fn) in zip(pre, parts):
        w = jnp.exp(base + c)
        if mask is not None:
            w = jnp.where(mask, w, 0.0)
        acc = acc + pv_fn(w.astype(bf16))
        c = c + tot
    c_ref[...] = c
    acc_ref[...] = acc


def _bias_chain(dist, rb_of):
    b = jnp.where(dist >= BUCKET_START[1], rb_of(1), rb_of(0))
    for k in range(2, N_BUCKETS):
        b = jnp.where(dist >= BUCKET_START[k], rb_of(k), b)
    return b


def _lam(lq1, lk1, lq2, lk2):
    return (jnp.exp(jnp.sum(lq1 * lk1, axis=1, keepdims=True))
            - jnp.exp(jnp.sum(lq2 * lk2, axis=1, keepdims=True)) + LAM_INIT)


def _sub_norm(od, sg):
    ms = jnp.mean(od * od, axis=-1, keepdims=True)
    return (od * lax.rsqrt(ms + NORM_EPS) * sg) * (1.0 - LAM_INIT)


def _nn(v):
    return lambda p: jnp.dot(p, v, preferred_element_type=f32)


TQ = 256
NQ = SEQ // TQ
META_TILE = 128
assert TQ + 1 >= FAR_DIST and N_META + TQ - (N_META - 1) >= FAR_DIST


def _attn_kernel(rb_ref, qd_ref, qs_ref, kd_ref, vd_ref, ks_ref, vs_ref,
                 kdm_ref, vdm_ref, ksm_ref, vsm_ref,
                 lq1_ref, lk1_ref, lq2_ref, lk2_ref, sg_ref, u_ref, um_ref,
                 od_ref, os_ref,
                 b0_ref, b1_ref, bm_ref, md_ref, ld_ref, accd_ref, cs_ref, accs_ref):
    h = pl.program_id(0)
    qi = pl.program_id(2)
    R = 2 * TQ

    def rb_of(k):
        return rb_ref[k, h]

    row = lax.broadcasted_iota(jnp.int32, (R, TQ), 0) % TQ
    col = lax.broadcasted_iota(jnp.int32, (R, TQ), 1)
    mrow = lax.broadcasted_iota(jnp.int32, (R, META_TILE), 0) % TQ
    mcol = lax.broadcasted_iota(jnp.int32, (R, META_TILE), 1)

    far_bias = rb_of(N_BUCKETS - 1)

    @pl.when(qi == 0)
    def _():
        b0_ref[...] = _bias_chain(row - col, rb_of) - far_bias
        b1_ref[...] = _bias_chain(TQ + row - col, rb_of) - far_bias
        bm_ref[...] = _bias_chain(N_META + mrow - mcol, rb_of) - far_bias

    @pl.when(qi == 1)
    def _():
        bm_ref[...] = jnp.zeros((R, META_TILE), f32)

    md_ref[...] = jnp.full((R, 1), NEG_INF, f32)
    ld_ref[...] = jnp.zeros((R, 1), f32)
    accd_ref[...] = jnp.zeros((R, 128), f32)
    cs_ref[...] = jnp.zeros((R, 1), f32)
    accs_ref[...] = jnp.zeros((R, 128), f32)

    lane = lax.broadcasted_iota(jnp.int32, (1, 128), 1)
    lo_half = (lane < HD_DIFF).astype(f32).astype(bf16)
    hi_half = (lane >= HD_DIFF).astype(f32).astype(bf16)
    qd = qd_ref[...]
    qs = qs_ref[...]
    qd2 = jnp.concatenate([qd * lo_half, qd * hi_half], axis=0)
    qs2 = jnp.concatenate([qs * lo_half, qs * hi_half], axis=0)

    def tiles(specs):
        dparts, sparts = [], []
        for j, bias, dmask, smask in specs:
            sl = pl.ds(pl.multiple_of(j * TQ, TQ), TQ)
            s = _nt_dot(qd2, kd_ref[sl, :])
            if bias is not None:
                s = s + bias
            if dmask is not None:
                s = jnp.where(dmask, s, NEG_INF)
            dparts.append((s, _nn(vd_ref[sl, :])))
            sparts.append((_nt_dot(qs2, ks_ref[sl, :]), smask, _nn(vs_ref[sl, :])))
        _diff_update(dparts, md_ref, ld_ref, accd_ref)
        _sb_update(sparts, u_ref, cs_ref, accs_ref)

    diag = (qi, b0_ref[...], row >= col, row > col)

    @pl.when(qi == 0)
    def _():
        tiles([diag])

    @pl.when(qi >= 1)
    def _():
        tiles([diag, (qi - 1, b1_ref[...], None, None)])

    n_far = qi - 1

    def far_pair(t, carry):
        j = qi - 2 - 2 * t
        tiles([(j, None, None, None), (j - 1, None, None, None)])
        return carry

    lax.fori_loop(0, n_far // 2, far_pair, 0)

    @pl.when(jnp.logical_and(qi >= 2, n_far % 2 == 1))
    def _():
        tiles([(0, None, None, None)])

    mvalid = mcol < N_META
    sm = jnp.where(mvalid, _nt_dot(qd2, kdm_ref[...]) + bm_ref[...], NEG_INF)
    _diff_update([(sm, _nn(vdm_ref[...]))], md_ref, ld_ref, accd_ref)
    _sb_update([(_nt_dot(qs2, ksm_ref[...]), mvalid, _nn(vsm_ref[...]))], um_ref, cs_ref, accs_ref)

    lam = _lam(lq1_ref[...], lk1_ref[...], lq2_ref[...], lk2_ref[...])
    o = accd_ref[...] / ld_ref[...]
    od = o[:TQ] - lam * o[TQ:]
    od_ref[...] = _sub_norm(od, sg_ref[...]).astype(bf16)
    accs = accs_ref[...]
    os_ref[...] = jnp.where(lane < HD_SB, accs[:TQ], accs[TQ:]).astype(bf16)


def _attn_call(rel_bias, q, kv16, lq1, lk1, lq2, lk2, sg, u, um):
    R = 2 * TQ
    nh = H_DIFF
    meta_blk = META_ROW0 // META_TILE
    assert META_ROW0 % META_TILE == 0

    def qspec(off):
        return pl.BlockSpec((TQ, 128), lambda h, b, i: (b * NQ + i, off + h))

    def kvspec(off):
        return pl.BlockSpec((SEQ, 128), lambda h, b, i: (b, off + h))

    def mspec(off):
        return pl.BlockSpec((META_TILE, 128), lambda h, b, i: (meta_blk, off + h))

    def const(shape):
        return pl.BlockSpec(shape, lambda h, b, i: (0, 0))

    out_spec = pl.BlockSpec((TQ, 128), lambda h, b, i: (b * NQ + i, h))
    return pl.pallas_call(
        _attn_kernel,
        grid=(nh, BATCH, NQ),
        in_specs=[pl.BlockSpec(memory_space=pltpu.SMEM),
                  qspec(0), qspec(nh),
                  kvspec(0), kvspec(nh), kvspec(2 * nh), kvspec(3 * nh),
                  mspec(0), mspec(nh), mspec(2 * nh), mspec(3 * nh),
                  const((1, HD_DIFF)), const((1, HD_DIFF)), const((1, HD_DIFF)), const((1, HD_DIFF)),
                  const((1, 128)), const((TQ, TQ)), const((META_TILE, META_TILE))],
        out_specs=[out_spec, out_spec],
        out_shape=[jax.ShapeDtypeStruct((ROWS_P, W_DIFF), bf16), jax.ShapeDtypeStruct((ROWS_P, W_SB), bf16)],
        scratch_shapes=[pltpu.VMEM((R, TQ), f32), pltpu.VMEM((R, TQ), f32), pltpu.VMEM((R, META_TILE), f32),
                        pltpu.VMEM((R, 1), f32), pltpu.VMEM((R, 1), f32), pltpu.VMEM((R, 128), f32),
                        pltpu.VMEM((R, 1), f32), pltpu.VMEM((R, 128), f32)],
        compiler_params=_params(("arbitrary", "arbitrary", "arbitrary")),
        name="attn_prompt",
    )(rel_bias, q, q, kv16, kv16, kv16, kv16, kv16, kv16, kv16, kv16, lq1, lk1, lq2, lk2, sg, u, um)


PPS = 4
NSTEP = N_PAGES // PPS
RS = 128
PH = PAGE_SIZE * H_DIFF
assert H_DIFF * 2 * DEC_SEQ == RS and H_SB * DEC_SEQ == RS
assert PAGE_SIZE + 1 >= FAR_DIST


def _diag_blocks_sb(pv):
    return jnp.concatenate([pv[8 * h:8 * (h + 1), 128 * (h // 2):128 * (h // 2 + 1)] for h in range(H_SB)], axis=0)


def _sample_kernel(pt_ref, rbrow_ref, qx_ref, qs_ref, kdn_ref, vdn_ref, ksn_ref, vsn_ref, *rest):
    cache_refs = rest[:4 * PPS]
    (lq1_ref, lk1_ref, lq2_ref, lk2_ref, sg_ref, u_ref, od_ref, os_ref,
     alast_ref, afar_ref, md_ref, ld_ref, accd_ref, cs_ref, accs_ref) = rest[4 * PPS:]
    s = pl.program_id(1)
    qx = qx_ref[...]
    qs = qs_ref[...]

    def rb_col(k):
        return rbrow_ref[:, k:k + 1]

    def diff_geometry(n_lanes, base):
        r = lax.broadcasted_iota(jnp.int32, (RS, n_lanes), 0)
        ln = lax.broadcasted_iota(jnp.int32, (RS, n_lanes), 1)
        same_head = (ln % H_DIFF) == (r // (2 * DEC_SEQ))
        dist = base + (r % DEC_SEQ) - (ln // H_DIFF)
        return same_head, dist

    def diff_part(kd, vd, add):
        return (_nt_dot(qx, kd.astype(bf16)) + add, _nn(vd.astype(bf16)))

    far_bias = rb_col(N_BUCKETS - 1)

    @pl.when(s == 0)
    def _():
        md_ref[...] = jnp.full((RS, 1), NEG_INF, f32)
        ld_ref[...] = jnp.zeros((RS, 1), f32)
        accd_ref[...] = jnp.zeros((RS, 128), f32)
        cs_ref[...] = jnp.zeros((RS, 1), f32)
        accs_ref[...] = jnp.zeros((RS, 128), f32)
        same, dist = diff_geometry(DEC_SEQ * H_DIFF, 0)
        add = jnp.where(same & (dist >= 0), _bias_chain(dist, rb_col) - far_bias, NEG_INF)
        _diff_update([diff_part(kdn_ref[...], vdn_ref[...], add)], md_ref, ld_ref, accd_ref)
        zpad = jnp.zeros((PAGE_SIZE - DEC_SEQ, COL), f32)
        ksn = jnp.concatenate([ksn_ref[...], zpad], axis=0).astype(bf16)
        vsn = jnp.concatenate([vsn_ref[...], zpad], axis=0).astype(bf16)
        qidx = lax.broadcasted_iota(jnp.int32, (RS, PAGE_SIZE), 0) % DEC_SEQ
        key = lax.broadcasted_iota(jnp.int32, (RS, PAGE_SIZE), 1)
        smask = (qidx > key) & (key < DEC_SEQ)
        _sb_update([(_nt_dot(qs, ksn), smask,
                     lambda w: _diag_blocks_sb(jnp.dot(w, vsn, preferred_element_type=f32)))],
                   u_ref, cs_ref, accs_ref)
        same, dist = diff_geometry(PH, PAGE_SIZE)
        alast_ref[...] = jnp.where(same, _bias_chain(dist, rb_col) - far_bias, NEG_INF)
        afar_ref[...] = jnp.where(same, 0.0, NEG_INF)

    @pl.when(s == 1)
    def _():
        alast_ref[...] = afar_ref[...]

    dparts, sparts = [], []
    for k in range(PPS):
        kd_ref, vd_ref, ks_ref, vs_ref = cache_refs[4 * k:4 * k + 4]
        dparts.append(diff_part(kd_ref[...], vd_ref[...], alast_ref[...] if k == 0 else afar_ref[...]))
        vt = vs_ref[...].astype(bf16)
        sparts.append((jnp.dot(qs, ks_ref[...].astype(bf16), preferred_element_type=f32), None,
                       lambda w, vt=vt: _diag_blocks_sb(_nt_dot(w, vt))))
    _diff_update(dparts, md_ref, ld_ref, accd_ref)
    _sb_update(sparts, u_ref, cs_ref, accs_ref)

    @pl.when(s == NSTEP - 1)
    def _():
        lam = _lam(lq1_ref[...], lk1_ref[...], lq2_ref[...], lk2_ref[...])
        o = accd_ref[...] / ld_ref[...]
        sg = sg_ref[...]
        heads = []
        for h in range(H_DIFF):
            od = o[16 * h:16 * h + 8] - lam * o[16 * h + 8:16 * h + 16]
            heads.append(_sub_norm(od, sg))
        od_ref[...] = jnp.concatenate(heads, axis=1).astype(bf16)
        accs = accs_ref[...]
        lane = lax.broadcasted_iota(jnp.int32, (1, 128), 1)
        pairs = [jnp.where(lane < HD_SB, accs[16 * t:16 * t + 8], accs[16 * t + 8:16 * t + 16])
                 for t in range(H_SB // 2)]
        os_ref[...] = jnp.concatenate(pairs, axis=1).astype(bf16)


def _sample_call(page_table, rbrow, qx, qbd_s, kdn, vdn, ksn, vsn, caches, lq1, lk1, lq2, lk2, sg, u):
    def const2(shape):
        return pl.BlockSpec(shape, lambda b, s, pt: (0, 0))

    def per_b(shape):
        return pl.BlockSpec((None,) + shape, lambda b, s, pt: (b, 0, 0))

    def cache_spec(k):
        return pl.BlockSpec((None, COL, PAGE_SIZE),
                            lambda b, s, pt: (pt[b, N_PAGES - 1 - (s * PPS + k)], 0, 0))

    in_specs = [const2((RS, N_BUCKETS)), per_b((RS, 128)), per_b((RS, COL)),
                per_b((DEC_SEQ * H_DIFF, 128)), per_b((DEC_SEQ * H_DIFF, 128)),
                per_b((DEC_SEQ, COL)), per_b((DEC_SEQ, COL))]
    args = [rbrow, qx, qbd_s, kdn, vdn, ksn, vsn]
    for k in range(PPS):
        for c in caches:
            in_specs.append(cache_spec(k))
            args.append(c)
    in_specs += [const2((1, HD_DIFF))] * 4 + [const2((1, 128)), const2((PAGE_SIZE, PAGE_SIZE))]
    args += [lq1, lk1, lq2, lk2, sg, u]
    out_spec = per_b((DEC_SEQ, COL))
    grid_spec = pltpu.PrefetchScalarGridSpec(
        num_scalar_prefetch=1,
        grid=(DEC_BATCH, NSTEP),
        in_specs=in_specs,
        out_specs=[out_spec, out_spec],
        scratch_shapes=[pltpu.VMEM((RS, PH), f32), pltpu.VMEM((RS, PH), f32),
                        pltpu.VMEM((RS, 1), f32), pltpu.VMEM((RS, 1), f32), pltpu.VMEM((RS, 128), f32),
                        pltpu.VMEM((RS, 1), f32), pltpu.VMEM((RS, 128), f32)],
    )
    return pl.pallas_call(
        _sample_kernel,
        grid_spec=grid_spec,
        out_shape=[jax.ShapeDtypeStruct((DEC_BATCH, DEC_SEQ, W_DIFF), bf16),
                   jax.ShapeDtypeStruct((DEC_BATCH, DEC_SEQ, W_SB), bf16)],
        compiler_params=_params(("arbitrary", "arbitrary")),
        name="attn_sample",
    )(page_table, *args)


TM_MIX = 256
NP_MIX = ROWS_P // TM_MIX


def _mix_kernel(odp_ref, osp_ref, ods_ref, oss_ref, g_ref, xp_ref, xs_ref,
                wa_ref, wb_ref, wo_ref, n2_ref, h2_ref, xn2_ref):
    i = pl.program_id(0)

    def body(od, osb, x):
        ya = jnp.dot(od, wa_ref[...], preferred_element_type=f32)
        yb = jnp.dot(osb, wb_ref[...], preferred_element_type=f32)
        g = g_ref[...]
        mix = g[:, :D_MODEL].astype(f32) * ya + g[:, D_MODEL:].astype(f32) * yb
        h2 = x + jnp.dot(mix.astype(bf16), wo_ref[...], preferred_element_type=f32)
        h2_ref[...] = h2
        xn2_ref[...] = _rms(h2, n2_ref[...]).astype(bf16)

    @pl.when(i < NP_MIX)
    def _():
        body(odp_ref[...], osp_ref[...], xp_ref[...])

    @pl.when(i >= NP_MIX)
    def _():
        body(ods_ref[...], oss_ref[...], xs_ref[...])


def _mix_call(od_p, os_p, od_s, os_s, gates, xp, xs, wa, wb, wo, n2):
    def prow(i):
        return (jnp.minimum(i, NP_MIX - 1), 0)

    def srow(i):
        return (jnp.maximum(i - NP_MIX, 0), 0)

    def full(shape):
        return pl.BlockSpec(shape, lambda i: (0, 0), pipeline_mode=pl.Buffered(1))

    return pl.pallas_call(
        _mix_kernel,
        grid=(ROWS // TM_MIX,),
        in_specs=[pl.BlockSpec((TM_MIX, W_DIFF), prow), pl.BlockSpec((TM_MIX, W_SB), prow),
                  pl.BlockSpec((TM_MIX, W_DIFF), srow), pl.BlockSpec((TM_MIX, W_SB), srow),
                  pl.BlockSpec((TM_MIX, 2 * D_MODEL), lambda i: (i, 0)),
                  pl.BlockSpec((TM_MIX, D_MODEL), prow), pl.BlockSpec((TM_MIX, D_MODEL), srow),
                  full((W_DIFF, D_MODEL)), full((W_SB, D_MODEL)), full((D_MODEL, D_MODEL)),
                  full((1, D_MODEL))],
        out_specs=[pl.BlockSpec((TM_MIX, D_MODEL), lambda i: (i, 0)),
                   pl.BlockSpec((TM_MIX, D_MODEL), lambda i: (i, 0))],
        out_shape=[jax.ShapeDtypeStruct((ROWS, D_MODEL), f32), jax.ShapeDtypeStruct((ROWS, D_MODEL), bf16)],
        compiler_params=_params(("arbitrary",)),
        name="mix_out",
    )(od_p, os_p, od_s, os_s, gates, xp, xs, wa, wb, wo, n2)


TM_MLP = 512
TF_MLP = 1024


def _mlp_kernel(xn_ref, h2_ref, wu_ref, wd_ref, o_ref):
    f = pl.program_id(1)
    u = jnp.maximum(jnp.dot(xn_ref[...], wu_ref[...], preferred_element_type=f32), 0.0)
    part = jnp.dot((u * u).astype(bf16), wd_ref[...], preferred_element_type=f32)

    @pl.when(f == 0)
    def _():
        o_ref[...] = h2_ref[...] + part

    @pl.when(f > 0)
    def _():
        o_ref[...] += part


def _mlp_call(xn2, h2, wu, wd):
    return pl.pallas_call(
        _mlp_kernel,
        grid=(ROWS // TM_MLP, D_FF // TF_MLP),
        in_specs=[pl.BlockSpec((TM_MLP, D_MODEL), lambda i, f: (i, 0)),
                  pl.BlockSpec((TM_MLP, D_MODEL), lambda i, f: (i, 0)),
                  pl.BlockSpec((D_MODEL, TF_MLP), lambda i, f: (0, f)),
                  pl.BlockSpec((TF_MLP, D_MODEL), lambda i, f: (f, 0))],
        out_specs=pl.BlockSpec((TM_MLP, D_MODEL), lambda i, f: (i, 0)),
        out_shape=jax.ShapeDtypeStruct((ROWS, D_MODEL), f32),
        compiler_params=_params(("arbitrary", "arbitrary")),
        name="mlp",
    )(xn2, h2, wu, wd)


def _later_matrix(n):
    r = np.arange(n)
    return jnp.asarray((r[:, None] > r[None, :]).astype(np.float32), dtype=bf16)


def _block_diag_queries(qrows, n_groups):
    gw = COL // n_groups
    grp = jnp.arange(COL, dtype=jnp.int32) // gw
    keep = grp[None, :] == jnp.arange(n_groups, dtype=jnp.int32)[:, None]
    out = jnp.where(keep[None, :, None, :], qrows[:, None, :, :], jnp.zeros((), qrows.dtype))
    return out.reshape(DEC_BATCH, n_groups * DEC_SEQ, COL)


def kernel(x_prompt, x_sample, cache_k_diff, cache_v_diff, cache_k_sb, cache_v_sb, page_table, meta_tokens,
           rel_bias, norm1_g, w_in, b_gate, qk_norm_q, qk_norm_k, lam_q1, lam_k1, lam_q2, lam_k2, subln_g,
           w_branch_a, w_branch_b, w_o, norm2_g, w_up, w_down):
    l = 0
    xp = x_prompt.reshape(ROWS_P, D_MODEL)
    xs = jnp.concatenate([x_sample.reshape(ROWS_S, D_MODEL), meta_tokens.astype(f32),
                          jnp.zeros((SMALL - ROWS_S - N_META, D_MODEL), f32)], axis=0)
    w_in16 = w_in[l].astype(bf16)
    wa16 = w_branch_a[l].astype(bf16)
    wb16 = w_branch_b[l].astype(bf16)
    wo16 = w_o[l].astype(bf16)
    wu16 = w_up[l].astype(bf16)
    wd16 = w_down[l].astype(bf16)
    gq = jnp.tile(qk_norm_q[l].astype(f32), COL // HD_DIFF)[None]
    gk = jnp.tile(qk_norm_k[l].astype(f32), COL // HD_DIFF)[None]
    grp = np.arange(256) // HD_DIFF
    gmat = jnp.asarray((grp[:, None] == grp[None, :]).astype(np.float32), dtype=bf16)
    lq1, lk1, lq2, lk2 = (a[l].astype(f32)[None] for a in (lam_q1, lam_k1, lam_q2, lam_k2))
    sg = subln_g[l].astype(f32)[None]
    rb = rel_bias.astype(f32)

    xn = _norm_call(xp, xs, norm1_g[l].astype(f32)[None])
    q, kv32, kv16, gates = _proj_calls(xn, w_in16, gq, gk, gmat, b_gate[l].astype(f32)[None])

    od_p, os_p = _attn_call(rb, q, kv16, lq1, lk1, lq2, lk2, sg, _later_matrix(TQ), _later_matrix(META_TILE))

    q_s = q[ROWS_P:ROWS_P + ROWS_S].reshape(DEC_BATCH, DEC_SEQ, 2 * COL)
    qd_s = jnp.transpose(q_s[..., :COL].reshape(DEC_BATCH, DEC_SEQ, H_DIFF, 2 * HD_DIFF), (0, 2, 1, 3))
    half = (jnp.arange(2 * HD_DIFF, dtype=jnp.int32) // HD_DIFF)[None, :] == jnp.arange(2, dtype=jnp.int32)[:, None]
    qx = jnp.where(half[None, None, :, None, :], qd_s[:, :, None], jnp.zeros((), bf16)).reshape(DEC_BATCH, RS, 128)
    qbd_s = _block_diag_queries(q_s[..., COL:], H_SB)
    new32 = kv32[ROWS_P:ROWS_P + ROWS_S].reshape(DEC_BATCH, DEC_SEQ, 4 * COL)
    kdn = new32[..., 0:COL].reshape(DEC_BATCH, DEC_SEQ * H_DIFF, 2 * HD_DIFF)
    vdn = new32[..., COL:2 * COL].reshape(DEC_BATCH, DEC_SEQ * H_DIFF, 2 * HD_DIFF)
    ksn = new32[..., 2 * COL:3 * COL]
    vsn = new32[..., 3 * COL:4 * COL]
    n_phys = cache_k_diff.shape[1]
    caches = [cache_k_diff[l].reshape(n_phys, COL, 2 * HD_DIFF), cache_v_diff[l].reshape(n_phys, COL, 2 * HD_DIFF),
              jnp.transpose(cache_k_sb[l], (0, 2, 3, 1)).reshape(n_phys, COL, PAGE_SIZE),
              jnp.transpose(cache_v_sb[l], (0, 2, 3, 1)).reshape(n_phys, COL, PAGE_SIZE)]
    rbrow = jnp.repeat(rb.T, 2 * DEC_SEQ, axis=0)
    od_s, os_s = _sample_call(page_table, rbrow, qx, qbd_s, kdn, vdn, ksn, vsn, caches, lq1, lk1, lq2, lk2, sg,
                              _later_matrix(PAGE_SIZE))
    pad = jnp.zeros((SMALL - ROWS_S, COL), bf16)
    od_s = jnp.concatenate([od_s.reshape(ROWS_S, COL), pad], axis=0)
    os_s = jnp.concatenate([os_s.reshape(ROWS_S, COL), pad], axis=0)

    h2, xn2 = _mix_call(od_p, os_p, od_s, os_s, gates, xp, xs, wa16, wb16, wo16, norm2_g[l].astype(f32)[None])
    y = _mlp_call(xn2, h2, wu16, wd16)

    y_prompt = y[:ROWS_P].reshape(BATCH, SEQ, D_MODEL)
    y_sample = y[ROWS_P:ROWS_P + ROWS_S].reshape(DEC_BATCH, DEC_SEQ, D_MODEL)

    def prompt_kv(c, heads, width):
        body = kv32[:ROWS_P, c * COL:(c + 1) * COL].reshape(BATCH, SEQ, heads, width)
        meta = kv32[META_ROW0:META_ROW0 + N_META, c * COL:(c + 1) * COL].reshape(1, N_META, heads, width)
        return jnp.concatenate([jnp.broadcast_to(meta, (BATCH, N_META, heads, width)), body], axis=1)[None]

    def sample_kv(c, heads, width):
        return kv32[ROWS_P:ROWS_P + ROWS_S, c * COL:(c + 1) * COL].reshape(1, DEC_BATCH, DEC_SEQ, heads, width)

    return (y_prompt, y_sample,
            prompt_kv(0, H_DIFF, 2 * HD_DIFF), prompt_kv(1, H_DIFF, 2 * HD_DIFF),
            prompt_kv(2, H_SB, HD_SB), prompt_kv(3, H_SB, HD_SB),
            sample_kv(0, H_DIFF, 2 * HD_DIFF), sample_kv(1, H_DIFF, 2 * HD_DIFF),
            sample_kv(2, H_SB, HD_SB), sample_kv(3, H_SB, HD_SB))
```

```python
import functools
import math

import numpy as np
import jax
import jax.numpy as jnp
from jax import lax
from jax.experimental import pallas as pl
from jax.experimental.pallas import tpu as pltpu

f32 = jnp.float32
bf16 = jnp.bfloat16

D_MODEL = 2048
BATCH = 4
SEQ = 2048
DEC_BATCH = 32
DEC_SEQ = 8
PAST_LEN = 8192
PAGE_SIZE = 128
N_PAGES = PAST_LEN // PAGE_SIZE
N_META = 16
H_DIFF = 8
HD_DIFF = 64
W_DIFF = H_DIFF * 2 * HD_DIFF
H_SB = 16
HD_SB = 64
W_SB = H_SB * HD_SB
D_FF = 4 * D_MODEL
N_BUCKETS = 32
MAX_DISTANCE = 128
NORM_EPS = 1e-6
NEG_INF = -1e30
LAM_INIT = 0.8 - 0.6 * math.exp(-0.3 * 0)

ROWS_P = BATCH * SEQ
ROWS_S = DEC_BATCH * DEC_SEQ
SMALL = 512
ROWS = ROWS_P + SMALL
META_ROW0 = ROWS_P + ROWS_S
COL = 1024
QSCALE = HD_DIFF ** -0.5

VMEM_LIMIT = 56 * 1024 * 1024


def _bucket_thresholds():
    n = np.arange(0, 4 * MAX_DISTANCE)
    max_exact = N_BUCKETS // 2
    nf = np.maximum(n, 1).astype(np.float64)
    large = max_exact + (np.log(nf / max_exact) / math.log(MAX_DISTANCE / max_exact)
                         * (N_BUCKETS - max_exact)).astype(np.int64)
    bucket = np.where(n < max_exact, n, np.minimum(large, N_BUCKETS - 1))
    return [int(np.argmax(bucket >= b)) for b in range(N_BUCKETS)]


BUCKET_START = _bucket_thresholds()
FAR_DIST = BUCKET_START[-1]


def _params(sem):
    return pltpu.CompilerParams(dimension_semantics=sem, vmem_limit_bytes=VMEM_LIMIT)


TM_NORM = 512
NP_TILES = ROWS_P // TM_NORM


def _rms(x, g):
    ms = jnp.mean(x * x, axis=-1, keepdims=True)
    return x * lax.rsqrt(ms + NORM_EPS) * g


def _norm_kernel(xp_ref, xs_ref, g_ref, o_ref):
    i = pl.program_id(0)

    @pl.when(i < NP_TILES)
    def _():
        o_ref[...] = _rms(xp_ref[...], g_ref[...]).astype(bf16)

    @pl.when(i >= NP_TILES)
    def _():
        o_ref[...] = _rms(xs_ref[...], g_ref[...]).astype(bf16)


def _norm_call(xp, xs, g):
    return pl.pallas_call(
        _norm_kernel,
        grid=(ROWS // TM_NORM,),
        in_specs=[
            pl.BlockSpec((TM_NORM, D_MODEL), lambda i: (jnp.minimum(i, NP_TILES - 1), 0)),
            pl.BlockSpec((TM_NORM, D_MODEL), lambda i: (jnp.maximum(i - NP_TILES, 0), 0)),
            pl.BlockSpec((1, D_MODEL), lambda i: (0, 0)),
        ],
        out_specs=pl.BlockSpec((TM_NORM, D_MODEL), lambda i: (i, 0)),
        out_shape=jax.ShapeDtypeStruct((ROWS, D_MODEL), bf16),
        compiler_params=_params(("arbitrary",)),
        name="norm1",
    )(xp, xs, g)


TM_PROJ = 1088
assert ROWS % TM_PROJ == 0


def _group_sumsq(z, gmat):
    zz = z * z
    hi = zz.astype(bf16)
    lo = (zz - hi.astype(f32)).astype(bf16)
    parts = []
    for c in range(z.shape[1] // 256):
        sl = slice(256 * c, 256 * (c + 1))
        parts.append(jnp.dot(hi[:, sl], gmat, preferred_element_type=f32)
                     + jnp.dot(lo[:, sl], gmat, preferred_element_type=f32))
    return jnp.concatenate(parts, axis=1)


def _qk_norm(z, g, gmat):
    ss = _group_sumsq(z, gmat)
    return z * lax.rsqrt(ss * (1.0 / HD_DIFF) + NORM_EPS) * g


def _proj_q_kernel(x_ref, w_ref, g_ref, gmat_ref, o_ref):
    j = pl.program_id(0)
    z = jnp.dot(x_ref[...], w_ref[...], preferred_element_type=f32)

    @pl.when(j == 0)
    def _():
        o_ref[...] = (_qk_norm(z, g_ref[...], gmat_ref[...]) * QSCALE).astype(bf16)

    @pl.when(j == 1)
    def _():
        o_ref[...] = (z * QSCALE).astype(bf16)


def _proj_kv_kernel(x_ref, w_ref, g_ref, gmat_ref, o32_ref, o16_ref):
    j = pl.program_id(0)
    z = jnp.dot(x_ref[...], w_ref[...], preferred_element_type=f32)

    @pl.when(j == 0)
    def _():
        zn = _qk_norm(z, g_ref[...], gmat_ref[...])
        o32_ref[...] = zn
        o16_ref[...] = zn.astype(bf16)

    @pl.when(j > 0)
    def _():
        o32_ref[...] = z
        o16_ref[...] = z.astype(bf16)


def _proj_gate_kernel(x_ref, w_ref, b_ref, o_ref):
    z = jnp.dot(x_ref[...], w_ref[...], preferred_element_type=f32) + b_ref[...]
    o_ref[...] = (1.0 / (1.0 + jnp.exp(-z))).astype(bf16)


def _proj_calls(xn, w_in, gq, gk, gmat, b_gate):
    n_rt = ROWS // TM_PROJ
    x_spec = pl.BlockSpec((TM_PROJ, D_MODEL), lambda j, i: (i, 0))
    vec_spec = pl.BlockSpec((1, COL), lambda j, i: (0, 0))
    gmat_spec = pl.BlockSpec((256, 256), lambda j, i: (0, 0))
    out_spec = pl.BlockSpec((TM_PROJ, COL), lambda j, i: (i, j))
    sem = ("arbitrary", "arbitrary")

    q = pl.pallas_call(
        _proj_q_kernel,
        grid=(2, n_rt),
        in_specs=[x_spec, pl.BlockSpec((D_MODEL, COL), lambda j, i: (0, 3 * j)), vec_spec, gmat_spec],
        out_specs=out_spec,
        out_shape=jax.ShapeDtypeStruct((ROWS, 2 * COL), bf16),
        compiler_params=_params(sem),
        name="proj_q",
    )(xn, w_in, gq, gmat)

    kv32, kv16 = pl.pallas_call(
        _proj_kv_kernel,
        grid=(4, n_rt),
        in_specs=[x_spec, pl.BlockSpec((D_MODEL, COL), lambda j, i: (0, j + 1 + j // 2)), vec_spec, gmat_spec],
        out_specs=[out_spec, out_spec],
        out_shape=[jax.ShapeDtypeStruct((ROWS, 4 * COL), f32), jax.ShapeDtypeStruct((ROWS, 4 * COL), bf16)],
        compiler_params=_params(sem),
        name="proj_kv",
    )(xn, w_in, gk, gmat)

    gates = pl.pallas_call(
        _proj_gate_kernel,
        grid=(4, n_rt),
        in_specs=[x_spec, pl.BlockSpec((D_MODEL, COL), lambda j, i: (0, j + 6)),
                  pl.BlockSpec((1, COL), lambda j, i: (0, j))],
        out_specs=out_spec,
        out_shape=jax.ShapeDtypeStruct((ROWS, 4 * COL), bf16),
        compiler_params=_params(sem),
        name="proj_gate",
    )(xn, w_in, b_gate)
    return q, kv32, kv16, gates


def _nt_dot(a, b):
    return lax.dot_general(a, b, (((1,), (1,)), ((), ())), preferred_element_type=f32)


def _diff_update(parts, m_ref, l_ref, acc_ref, key_axis):
    m_old = m_ref[...]
    m_new = m_old
    for s, _ in parts:
        m_new = jnp.maximum(m_new, jnp.max(s, axis=key_axis, keepdims=True))
    alpha = jnp.exp(m_old - m_new)
    l_new = alpha * l_ref[...]
    acc = alpha * acc_ref[...]
    for s, pv_fn in parts:
        p = jnp.exp(s - m_new)
        l_new = l_new + jnp.sum(p, axis=key_axis, keepdims=True)
        acc = acc + pv_fn(p.astype(bf16))
    l_ref[...] = l_new
    acc_ref[...] = acc
    m_ref[...] = m_new


SIGN_BIT = -2 ** 31
SB_CUTOFF = 105.0


def _sb_update(parts, later_fn, c_ref, acc_ref, key_axis):
    pre = []
    for z, mask, _ in parts:
        neg_abs = lax.bitcast_convert_type(lax.bitcast_convert_type(z, jnp.int32) | SIGN_BIT, f32)
        sp = jnp.maximum(z, 0.0) + jnp.log(1.0 + jnp.exp(neg_abs))
        base = z - sp
        if mask is not None:
            sp = jnp.where(mask, sp, 0.0)
        hi = sp.astype(bf16)
        lo = (sp - hi.astype(f32)).astype(bf16)
        pre.append((base - (later_fn(hi) + later_fn(lo)), jnp.sum(sp, axis=key_axis, keepdims=True)))
    c = c_ref[...]
    acc = acc_ref[...]
    for (base, tot), (_, mask, pv_fn) in zip(pre, parts):
        w = jnp.exp(base - c)
        if mask is not None:
            w = jnp.where(mask, w, 0.0)
        acc = acc + pv_fn(w.astype(bf16))
        c = c + tot
    c_ref[...] = c
    acc_ref[...] = acc


def _bias_chain(dist, rb_of):
    b = jnp.where(dist >= BUCKET_START[1], rb_of(1), rb_of(0))
    for k in range(2, N_BUCKETS):
        b = jnp.where(dist >= BUCKET_START[k], rb_of(k), b)
    return b


def _lam(lq1, lk1, lq2, lk2):
    return (jnp.exp(jnp.sum(lq1 * lk1, axis=1, keepdims=True))
            - jnp.exp(jnp.sum(lq2 * lk2, axis=1, keepdims=True)) + LAM_INIT)


def _sub_norm(od, sg):
    ms = jnp.mean(od * od, axis=-1, keepdims=True)
    return (od * lax.rsqrt(ms + NORM_EPS) * sg) * (1.0 - LAM_INIT)


def _nn(v):
    return lambda p: jnp.dot(p, v, preferred_element_type=f32)


TQ = 256
NQ = SEQ // TQ
META_TILE = 128
assert TQ + 1 >= FAR_DIST and N_META + TQ - (N_META - 1) >= FAR_DIST


def _left(vt):
    return lambda p: jnp.dot(vt, p, preferred_element_type=f32)


def _transpose_bf16(x):
    return x.astype(f32).T.astype(bf16)


def _attn_kernel(rb_ref, qd_ref, qs_ref, kd_ref, vd_ref, ks_ref, vs_ref,
                 kdm_ref, vdm_ref, ksm_ref, vsm_ref,
                 lq1_ref, lk1_ref, lq2_ref, lk2_ref, sg_ref, a_ref, am_ref,
                 od_ref, os_ref,
                 b0_ref, b1_ref, bm_ref, vdt_ref, vst_ref, vdmt_ref, vsmt_ref,
                 md_ref, ld_ref, accd_ref, cs_ref, accs_ref):
    h = pl.program_id(0)
    qi = pl.program_id(2)
    R = 2 * TQ

    def rb_of(k):
        return rb_ref[k, h]

    key = lax.broadcasted_iota(jnp.int32, (TQ, R), 0)
    qry = lax.broadcasted_iota(jnp.int32, (TQ, R), 1) % TQ

    far_bias = rb_of(N_BUCKETS - 1)

    @pl.when(jnp.logical_and(pl.program_id(1) == 0, qi == 0))
    def _():
        b0_ref[...] = _bias_chain(qry - key, rb_of) - far_bias
        b1_ref[...] = _bias_chain(TQ + qry - key, rb_of) - far_bias
        mkey = lax.broadcasted_iota(jnp.int32, (N_META, R), 0)
        mqry = lax.broadcasted_iota(jnp.int32, (N_META, R), 1) % TQ
        bm_ref[...] = _bias_chain(N_META + mqry - mkey, rb_of) - far_bias

    @pl.when(qi == 0)
    def _():
        for j in range(NQ):
            vdt_ref[j] = _transpose_bf16(vd_ref[j * TQ:(j + 1) * TQ, :])
            vst_ref[j] = _transpose_bf16(vs_ref[j * TQ:(j + 1) * TQ, :])
        vdmt_ref[...] = _transpose_bf16(vdm_ref[...])
        vsmt_ref[...] = _transpose_bf16(vsm_ref[...])

    md_ref[...] = jnp.full((1, R), NEG_INF, f32)
    ld_ref[...] = jnp.zeros((1, R), f32)
    accd_ref[...] = jnp.zeros((128, R), f32)
    cs_ref[...] = jnp.zeros((1, R), f32)
    accs_ref[...] = jnp.zeros((128, R), f32)

    lane = lax.broadcasted_iota(jnp.int32, (1, 128), 1)
    lo_half = (lane < HD_DIFF).astype(f32).astype(bf16)
    hi_half = (lane >= HD_DIFF).astype(f32).astype(bf16)
    qd = qd_ref[...]
    qs = qs_ref[...]
    qd2 = jnp.concatenate([qd * lo_half, qd * hi_half], axis=0)
    qs2 = jnp.concatenate([qs * lo_half, qs * hi_half], axis=0)
    a = a_ref[...]

    def later(x):
        return jnp.dot(a, x, preferred_element_type=f32)

    def tiles(specs, with_sb, extra_diff=()):
        dparts = list(extra_diff)
        sparts = []
        for j, bias, dmask, smask in specs:
            sl = pl.ds(pl.multiple_of(j * TQ, TQ), TQ)
            s = _nt_dot(kd_ref[sl, :], qd2)
            if bias is not None:
                s = s + bias
            if dmask is not None:
                s = jnp.where(dmask, s, NEG_INF)
            dparts.append((s, _left(vdt_ref[j])))
            if with_sb:
                sparts.append((_nt_dot(ks_ref[sl, :], qs2), smask, _left(vst_ref[j])))
        _diff_update(dparts, md_ref, ld_ref, accd_ref, 0)
        if with_sb:
            _sb_update(sparts, later, cs_ref, accs_ref, 0)

    def far(js, with_sb):
        tiles([(j, None, None, None) for j in js], with_sb)

    meta_bias = bm_ref[...] * jnp.where(qi == 0, 1.0, 0.0)
    meta_diff = (_nt_dot(kdm_ref[0:N_META, :], qd2) + meta_bias, _left(vdmt_ref[:, 0:N_META]))
    diag = (qi, b0_ref[...], qry >= key, qry > key)

    @pl.when(qi == 0)
    def _():
        tiles([diag], True, [meta_diff])

    @pl.when(qi >= 1)
    def _():
        tiles([diag, (qi - 1, b1_ref[...], None, None)], True, [meta_diff])

    sb_live = jnp.min(cs_ref[...], axis=1, keepdims=True)[0, 0] < SB_CUTOFF
    n_far = jnp.maximum(qi - 1, 0)

    @pl.when(sb_live)
    def _():
        def pair(t, carry):
            j = qi - 2 - 2 * t
            far([j, j - 1], True)
            return carry

        lax.fori_loop(0, n_far // 2, pair, 0)

        @pl.when(n_far % 2 == 1)
        def _():
            far([0], True)

        am = am_ref[...]
        _sb_update([(_nt_dot(ksm_ref[0:N_META, :], qs2), None, _left(vsmt_ref[:, 0:N_META]))],
                   lambda x: jnp.dot(am, x, preferred_element_type=f32), cs_ref, accs_ref, 0)

    @pl.when(jnp.logical_not(sb_live))
    def _():
        def quad(t, carry):
            j = qi - 2 - 4 * t
            far([j, j - 1, j - 2, j - 3], False)
            return carry

        lax.fori_loop(0, n_far // 4, quad, 0)

        def single(t, carry):
            far([n_far % 4 - 1 - t], False)
            return carry

        lax.fori_loop(0, n_far % 4, single, 0)

    lam = _lam(lq1_ref[...], lk1_ref[...], lq2_ref[...], lk2_ref[...])
    o = accd_ref[...] / ld_ref[...]
    odt = o[:, :TQ] - lam * o[:, TQ:]
    od_ref[...] = _sub_norm(odt.T, sg_ref[...]).astype(bf16)
    accs = accs_ref[...]
    vrow = lax.broadcasted_iota(jnp.int32, (128, TQ), 0)
    os_ref[...] = jnp.where(vrow < HD_SB, accs[:, :TQ], accs[:, TQ:]).T.astype(bf16)


def _attn_call(rel_bias, q, kv16, lq1, lk1, lq2, lk2, sg, a, am):
    R = 2 * TQ
    nh = H_DIFF
    meta_blk = META_ROW0 // META_TILE
    assert META_ROW0 % META_TILE == 0

    def qspec(off):
        return pl.BlockSpec((TQ, 128), lambda h, b, i: (b * NQ + i, off + h))

    def kvspec(off):
        return pl.BlockSpec((SEQ, 128), lambda h, b, i: (b, off + h))

    def mspec(off):
        return pl.BlockSpec((META_TILE, 128), lambda h, b, i: (meta_blk, off + h))

    def const(shape):
        return pl.BlockSpec(shape, lambda h, b, i: (0, 0))

    out_spec = pl.BlockSpec((TQ, 128), lambda h, b, i: (b * NQ + i, h))
    return pl.pallas_call(
        _attn_kernel,
        grid=(nh, BATCH, NQ),
        in_specs=[pl.BlockSpec(memory_space=pltpu.SMEM),
                  qspec(0), qspec(nh),
                  kvspec(0), kvspec(nh), kvspec(2 * nh), kvspec(3 * nh),
                  mspec(0), mspec(nh), mspec(2 * nh), mspec(3 * nh),
                  const((1, HD_DIFF)), const((1, HD_DIFF)), const((1, HD_DIFF)), const((1, HD_DIFF)),
                  const((1, 128)), const((TQ, TQ)), const((N_META, N_META))],
        out_specs=[out_spec, out_spec],
        out_shape=[jax.ShapeDtypeStruct((ROWS_P, W_DIFF), bf16), jax.ShapeDtypeStruct((ROWS_P, W_SB), bf16)],
        scratch_shapes=[pltpu.VMEM((TQ, R), f32), pltpu.VMEM((TQ, R), f32), pltpu.VMEM((N_META, R), f32),
                        pltpu.VMEM((NQ, 128, TQ), bf16), pltpu.VMEM((NQ, 128, TQ), bf16),
                        pltpu.VMEM((128, META_TILE), bf16), pltpu.VMEM((128, META_TILE), bf16),
                        pltpu.VMEM((1, R), f32), pltpu.VMEM((1, R), f32), pltpu.VMEM((128, R), f32),
                        pltpu.VMEM((1, R), f32), pltpu.VMEM((128, R), f32)],
        compiler_params=_params(("arbitrary", "arbitrary", "arbitrary")),
        name="attn_prompt",
    )(rel_bias, q, q, kv16, kv16, kv16, kv16, kv16, kv16, kv16, kv16, lq1, lk1, lq2, lk2, sg, a, am)


PPS = 8
PGROUP = 4
NSTEP = N_PAGES // PPS
assert N_PAGES % PPS == 0 and PPS % PGROUP == 0
RS = 128
PH = PAGE_SIZE * H_DIFF
assert H_DIFF * 2 * DEC_SEQ == RS and H_SB * DEC_SEQ == RS
assert PAGE_SIZE + 1 >= FAR_DIST


def _diag_blocks_sb(pv):
    return jnp.concatenate([pv[8 * h:8 * (h + 1), 128 * (h // 2):128 * (h // 2 + 1)] for h in range(H_SB)], axis=0)


def _sample_kernel(pt_ref, rbrow_ref, qx_ref, qs_ref, kdn_ref, vdn_ref, ksn_ref, vsn_ref, *rest):
    cache_refs = rest[:4 * PPS]
    (lq1_ref, lk1_ref, lq2_ref, lk2_ref, sg_ref, u_ref, od_ref, os_ref,
     alast_ref, afar_ref, md_ref, ld_ref, accd_ref, cs_ref, accs_ref) = rest[4 * PPS:]
    s = pl.program_id(1)
    qx = qx_ref[...]
    qs = qs_ref[...]

    def rb_col(k):
        return rbrow_ref[:, k:k + 1]

    def diff_geometry(n_lanes, base):
        r = lax.broadcasted_iota(jnp.int32, (RS, n_lanes), 0)
        ln = lax.broadcasted_iota(jnp.int32, (RS, n_lanes), 1)
        same_head = (ln % H_DIFF) == (r // (2 * DEC_SEQ))
        dist = base + (r % DEC_SEQ) - (ln // H_DIFF)
        return same_head, dist

    def diff_part(kd, vd, add):
        return (_nt_dot(qx, kd.astype(bf16)) + add, _nn(vd.astype(bf16)))

    u = u_ref[...]

    def later(x):
        return jnp.dot(x, u, preferred_element_type=f32)

    far_bias = rb_col(N_BUCKETS - 1)

    @pl.when(s == 0)
    def _():
        md_ref[...] = jnp.full((RS, 1), NEG_INF, f32)
        ld_ref[...] = jnp.zeros((RS, 1), f32)
        accd_ref[...] = jnp.zeros((RS, 128), f32)
        cs_ref[...] = jnp.zeros((RS, 1), f32)
        accs_ref[...] = jnp.zeros((RS, 128), f32)
        same, dist = diff_geometry(DEC_SEQ * H_DIFF, 0)
        add = jnp.where(same & (dist >= 0), _bias_chain(dist, rb_col) - far_bias, NEG_INF)
        _diff_update([diff_part(kdn_ref[...], vdn_ref[...], add)], md_ref, ld_ref, accd_ref, 1)
        zpad = jnp.zeros((PAGE_SIZE - DEC_SEQ, COL), f32)
        ksn = jnp.concatenate([ksn_ref[...], zpad], axis=0).astype(bf16)
        vsn = jnp.concatenate([vsn_ref[...], zpad], axis=0).astype(bf16)
        qidx = lax.broadcasted_iota(jnp.int32, (RS, PAGE_SIZE), 0) % DEC_SEQ
        key = lax.broadcasted_iota(jnp.int32, (RS, PAGE_SIZE), 1)
        smask = (qidx > key) & (key < DEC_SEQ)
        _sb_update([(_nt_dot(qs, ksn), smask,
                     lambda w: _diag_blocks_sb(jnp.dot(w, vsn, preferred_element_type=f32)))],
                   later, cs_ref, accs_ref, 1)
        same, dist = diff_geometry(PH, PAGE_SIZE)
        alast_ref[...] = jnp.where(same, _bias_chain(dist, rb_col) - far_bias, NEG_INF)
        afar_ref[...] = jnp.where(same, 0.0, NEG_INF)

    @pl.when(s == 1)
    def _():
        alast_ref[...] = afar_ref[...]

    for g in range(PPS // PGROUP):
        pages = range(g * PGROUP, (g + 1) * PGROUP)
        dparts = []
        for k in pages:
            kd_ref, vd_ref = cache_refs[4 * k:4 * k + 2]
            dparts.append(diff_part(kd_ref[...], vd_ref[...], alast_ref[...] if k == 0 else afar_ref[...]))
        _diff_update(dparts, md_ref, ld_ref, accd_ref, 1)

        @pl.when(jnp.min(cs_ref[...], axis=0, keepdims=True)[0, 0] < SB_CUTOFF)
        def _():
            sparts = []
            for k in pages:
                ks_ref, vs_ref = cache_refs[4 * k + 2:4 * k + 4]
                vt = vs_ref[...].astype(bf16)
                sparts.append((jnp.dot(qs, ks_ref[...].astype(bf16), preferred_element_type=f32), None,
                               lambda w, vt=vt: _diag_blocks_sb(_nt_dot(w, vt))))
            _sb_update(sparts, later, cs_ref, accs_ref, 1)

    @pl.when(s == NSTEP - 1)
    def _():
        lam = _lam(lq1_ref[...], lk1_ref[...], lq2_ref[...], lk2_ref[...])
        o = accd_ref[...] / ld_ref[...]
        sg = sg_ref[...]
        heads = []
        for h in range(H_DIFF):
            od = o[16 * h:16 * h + 8] - lam * o[16 * h + 8:16 * h + 16]
            heads.append(_sub_norm(od, sg))
        od_ref[...] = jnp.concatenate(heads, axis=1).astype(bf16)
        accs = accs_ref[...]
        lane = lax.broadcasted_iota(jnp.int32, (1, 128), 1)
        pairs = [jnp.where(lane < HD_SB, accs[16 * t:16 * t + 8], accs[16 * t + 8:16 * t + 16])
                 for t in range(H_SB // 2)]
        os_ref[...] = jnp.concatenate(pairs, axis=1).astype(bf16)


def _sample_call(page_table, rbrow, qx, qbd_s, kdn, vdn, ksn, vsn, caches, lq1, lk1, lq2, lk2, sg, u):
    def const2(shape):
        return pl.BlockSpec(shape, lambda b, s, pt: (0, 0))

    def per_b(shape):
        return pl.BlockSpec((None,) + shape, lambda b, s, pt: (b, 0, 0))

    def cache_spec(k):
        return pl.BlockSpec((None, COL, PAGE_SIZE),
                            lambda b, s, pt: (pt[b, N_PAGES - 1 - (s * PPS + k)], 0, 0))

    in_specs = [const2((RS, N_BUCKETS)), per_b((RS, 128)), per_b((RS, COL)),
                per_b((DEC_SEQ * H_DIFF, 128)), per_b((DEC_SEQ * H_DIFF, 128)),
                per_b((DEC_SEQ, COL)), per_b((DEC_SEQ, COL))]
    args = [rbrow, qx, qbd_s, kdn, vdn, ksn, vsn]
    for k in range(PPS):
        for c in caches:
            in_specs.append(cache_spec(k))
            args.append(c)
    in_specs += [const2((1, HD_DIFF))] * 4 + [const2((1, 128)), const2((PAGE_SIZE, PAGE_SIZE))]
    args += [lq1, lk1, lq2, lk2, sg, u]
    out_spec = per_b((DEC_SEQ, COL))
    grid_spec = pltpu.PrefetchScalarGridSpec(
        num_scalar_prefetch=1,
        grid=(DEC_BATCH, NSTEP),
        in_specs=in_specs,
        out_specs=[out_spec, out_spec],
        scratch_shapes=[pltpu.VMEM((RS, PH), f32), pltpu.VMEM((RS, PH), f32),
                        pltpu.VMEM((RS, 1), f32), pltpu.VMEM((RS, 1), f32), pltpu.VMEM((RS, 128), f32),
                        pltpu.VMEM((RS, 1), f32), pltpu.VMEM((RS, 128), f32)],
    )
    return pl.pallas_call(
        _sample_kernel,
        grid_spec=grid_spec,
        out_shape=[jax.ShapeDtypeStruct((DEC_BATCH, DEC_SEQ, W_DIFF), bf16),
                   jax.ShapeDtypeStruct((DEC_BATCH, DEC_SEQ, W_SB), bf16)],
        compiler_params=_params(("arbitrary", "arbitrary")),
        name="attn_sample",
    )(page_table, *args)


TM_MIX = 256
NP_MIX = ROWS_P // TM_MIX


def _mix_kernel(odp_ref, osp_ref, ods_ref, oss_ref, g_ref, xp_ref, xs_ref,
                wa_ref, wb_ref, wo_ref, n2_ref, h2_ref, xn2_ref):
    i = pl.program_id(0)

    def body(od, osb, x):
        ya = jnp.dot(od, wa_ref[...], preferred_element_type=f32)
        yb = jnp.dot(osb, wb_ref[...], preferred_element_type=f32)
        g = g_ref[...]
        mix = g[:, :D_MODEL].astype(f32) * ya + g[:, D_MODEL:].astype(f32) * yb
        h2 = x + jnp.dot(mix.astype(bf16), wo_ref[...], preferred_element_type=f32)
        h2_ref[...] = h2
        xn2_ref[...] = _rms(h2, n2_ref[...]).astype(bf16)

    @pl.when(i < NP_MIX)
    def _():
        body(odp_ref[...], osp_ref[...], xp_ref[...])

    @pl.when(i >= NP_MIX)
    def _():
        body(ods_ref[...], oss_ref[...], xs_ref[...])


def _mix_call(od_p, os_p, od_s, os_s, gates, xp, xs, wa, wb, wo, n2):
    def prow(i):
        return (jnp.minimum(i, NP_MIX - 1), 0)

    def srow(i):
        return (jnp.maximum(i - NP_MIX, 0), 0)

    def full(shape):
        return pl.BlockSpec(shape, lambda i: (0, 0), pipeline_mode=pl.Buffered(1))

    return pl.pallas_call(
        _mix_kernel,
        grid=(ROWS // TM_MIX,),
        in_specs=[pl.BlockSpec((TM_MIX, W_DIFF), prow), pl.BlockSpec((TM_MIX, W_SB), prow),
                  pl.BlockSpec((TM_MIX, W_DIFF), srow), pl.BlockSpec((TM_MIX, W_SB), srow),
                  pl.BlockSpec((TM_MIX, 2 * D_MODEL), lambda i: (i, 0)),
                  pl.BlockSpec((TM_MIX, D_MODEL), prow), pl.BlockSpec((TM_MIX, D_MODEL), srow),
                  full((W_DIFF, D_MODEL)), full((W_SB, D_MODEL)), full((D_MODEL, D_MODEL)),
                  full((1, D_MODEL))],
        out_specs=[pl.BlockSpec((TM_MIX, D_MODEL), lambda i: (i, 0)),
                   pl.BlockSpec((TM_MIX, D_MODEL), lambda i: (i, 0))],
        out_shape=[jax.ShapeDtypeStruct((ROWS, D_MODEL), f32), jax.ShapeDtypeStruct((ROWS, D_MODEL), bf16)],
        compiler_params=_params(("arbitrary",)),
        name="mix_out",
    )(od_p, os_p, od_s, os_s, gates, xp, xs, wa, wb, wo, n2)


TM_MLP = 512
TF_MLP = 1024


def _mlp_kernel(xn_ref, h2_ref, wu_ref, wd_ref, o_ref):
    f = pl.program_id(1)
    u = jnp.maximum(jnp.dot(xn_ref[...], wu_ref[...], preferred_element_type=f32), 0.0)
    part = jnp.dot((u * u).astype(bf16), wd_ref[...], preferred_element_type=f32)

    @pl.when(f == 0)
    def _():
        o_ref[...] = h2_ref[...] + part

    @pl.when(f > 0)
    def _():
        o_ref[...] += part


def _mlp_call(xn2, h2, wu, wd):
    return pl.pallas_call(
        _mlp_kernel,
        grid=(ROWS // TM_MLP, D_FF // TF_MLP),
        in_specs=[pl.BlockSpec((TM_MLP, D_MODEL), lambda i, f: (i, 0)),
                  pl.BlockSpec((TM_MLP, D_MODEL), lambda i, f: (i, 0)),
                  pl.BlockSpec((D_MODEL, TF_MLP), lambda i, f: (0, f)),
                  pl.BlockSpec((TF_MLP, D_MODEL), lambda i, f: (f, 0))],
        out_specs=pl.BlockSpec((TM_MLP, D_MODEL), lambda i, f: (i, 0)),
        out_shape=jax.ShapeDtypeStruct((ROWS, D_MODEL), f32),
        compiler_params=_params(("arbitrary", "arbitrary")),
        name="mlp",
    )(xn2, h2, wu, wd)


def _later_matrix(n, keys_on_rows=False):
    r = np.arange(n)
    m = r[:, None] > r[None, :]
    return jnp.asarray((m.T if keys_on_rows else m).astype(np.float32), dtype=bf16)


def _block_diag_queries(qrows, n_groups):
    gw = COL // n_groups
    grp = jnp.arange(COL, dtype=jnp.int32) // gw
    keep = grp[None, :] == jnp.arange(n_groups, dtype=jnp.int32)[:, None]
    out = jnp.where(keep[None, :, None, :], qrows[:, None, :, :], jnp.zeros((), qrows.dtype))
    return out.reshape(DEC_BATCH, n_groups * DEC_SEQ, COL)


def kernel(x_prompt, x_sample, cache_k_diff, cache_v_diff, cache_k_sb, cache_v_sb, page_table, meta_tokens,
           rel_bias, norm1_g, w_in, b_gate, qk_norm_q, qk_norm_k, lam_q1, lam_k1, lam_q2, lam_k2, subln_g,
           w_branch_a, w_branch_b, w_o, norm2_g, w_up, w_down):
    l = 0
    xp = x_prompt.reshape(ROWS_P, D_MODEL)
    xs = jnp.concatenate([x_sample.reshape(ROWS_S, D_MODEL), meta_tokens.astype(f32),
                          jnp.zeros((SMALL - ROWS_S - N_META, D_MODEL), f32)], axis=0)
    w_in16 = w_in[l].astype(bf16)
    wa16 = w_branch_a[l].astype(bf16)
    wb16 = w_branch_b[l].astype(bf16)
    wo16 = w_o[l].astype(bf16)
    wu16 = w_up[l].astype(bf16)
    wd16 = w_down[l].astype(bf16)
    gq = jnp.tile(qk_norm_q[l].astype(f32), COL // HD_DIFF)[None]
    gk = jnp.tile(qk_norm_k[l].astype(f32), COL // HD_DIFF)[None]
    grp = np.arange(256) // HD_DIFF
    gmat = jnp.asarray((grp[:, None] == grp[None, :]).astype(np.float32), dtype=bf16)
    lq1, lk1, lq2, lk2 = (a[l].astype(f32)[None] for a in (lam_q1, lam_k1, lam_q2, lam_k2))
    sg = subln_g[l].astype(f32)[None]
    rb = rel_bias.astype(f32)

    xn = _norm_call(xp, xs, norm1_g[l].astype(f32)[None])
    q, kv32, kv16, gates = _proj_calls(xn, w_in16, gq, gk, gmat, b_gate[l].astype(f32)[None])

    od_p, os_p = _attn_call(rb, q, kv16, lq1, lk1, lq2, lk2, sg,
                            _later_matrix(TQ, keys_on_rows=True), _later_matrix(N_META, keys_on_rows=True))

    q_s = q[ROWS_P:ROWS_P + ROWS_S].reshape(DEC_BATCH, DEC_SEQ, 2 * COL)
    qd_s = jnp.transpose(q_s[..., :COL].reshape(DEC_BATCH, DEC_SEQ, H_DIFF, 2 * HD_DIFF), (0, 2, 1, 3))
    half = (jnp.arange(2 * HD_DIFF, dtype=jnp.int32) // HD_DIFF)[None, :] == jnp.arange(2, dtype=jnp.int32)[:, None]
    qx = jnp.where(half[None, None, :, None, :], qd_s[:, :, None], jnp.zeros((), bf16)).reshape(DEC_BATCH, RS, 128)
    qbd_s = _block_diag_queries(q_s[..., COL:], H_SB)
    new32 = kv32[ROWS_P:ROWS_P + ROWS_S].reshape(DEC_BATCH, DEC_SEQ, 4 * COL)
    kdn = new32[..., 0:COL].reshape(DEC_BATCH, DEC_SEQ * H_DIFF, 2 * HD_DIFF)
    vdn = new32[..., COL:2 * COL].reshape(DEC_BATCH, DEC_SEQ * H_DIFF, 2 * HD_DIFF)
    ksn = new32[..., 2 * COL:3 * COL]
    vsn = new32[..., 3 * COL:4 * COL]
    n_phys = cache_k_diff.shape[1]
    caches = [cache_k_diff[l].reshape(n_phys, COL, 2 * HD_DIFF), cache_v_diff[l].reshape(n_phys, COL, 2 * HD_DIFF),
              jnp.transpose(cache_k_sb[l], (0, 2, 3, 1)).reshape(n_phys, COL, PAGE_SIZE),
              jnp.transpose(cache_v_sb[l], (0, 2, 3, 1)).reshape(n_phys, COL, PAGE_SIZE)]
    rbrow = jnp.repeat(rb.T, 2 * DEC_SEQ, axis=0)
    od_s, os_s = _sample_call(page_table, rbrow, qx, qbd_s, kdn, vdn, ksn, vsn, caches, lq1, lk1, lq2, lk2, sg,
                              _later_matrix(PAGE_SIZE))
    pad = jnp.zeros((SMALL - ROWS_S, COL), bf16)
    od_s = jnp.concatenate([od_s.reshape(ROWS_S, COL), pad], axis=0)
    os_s = jnp.concatenate([os_s.reshape(ROWS_S, COL), pad], axis=0)

    h2, xn2 = _mix_call(od_p, os_p, od_s, os_s, gates, xp, xs, wa16, wb16, wo16, norm2_g[l].astype(f32)[None])
    y = _mlp_call(xn2, h2, wu16, wd16)

    y_prompt = y[:ROWS_P].reshape(BATCH, SEQ, D_MODEL)
    y_sample = y[ROWS_P:ROWS_P + ROWS_S].reshape(DEC_BATCH, DEC_SEQ, D_MODEL)

    def prompt_kv(c, heads, width):
        body = kv32[:ROWS_P, c * COL:(c + 1) * COL].reshape(BATCH, SEQ, heads, width)
        meta = kv32[META_ROW0:META_ROW0 + N_META, c * COL:(c + 1) * COL].reshape(1, N_META, heads, width)
        return jnp.concatenate([jnp.broadcast_to(meta, (BATCH, N_META, heads, width)), body], axis=1)[None]

    def sample_kv(c, heads, width):
        return kv32[ROWS_P:ROWS_P + ROWS_S, c * COL:(c + 1) * COL].reshape(1, DEC_BATCH, DEC_SEQ, heads, width)

    return (y_prompt, y_sample,
            prompt_kv(0, H_DIFF, 2 * HD_DIFF), prompt_kv(1, H_DIFF, 2 * HD_DIFF),
            prompt_kv(2, H_SB, HD_SB), prompt_kv(3, H_SB, HD_SB),
            sample_kv(0, H_DIFF, 2 * HD_DIFF), sample_kv(1, H_DIFF, 2 * HD_DIFF),
            sample_kv(2, H_SB, HD_SB), sample_kv(3, H_SB, HD_SB))
```

```python
import functools
import math

import numpy as np
import jax
import jax.numpy as jnp
from jax import lax
from jax.experimental import pallas as pl
from jax.experimental.pallas import tpu as pltpu

f32 = jnp.float32
bf16 = jnp.bfloat16

D_MODEL = 2048
BATCH = 4
SEQ = 2048
DEC_BATCH = 32
DEC_SEQ = 8
PAST_LEN = 8192
PAGE_SIZE = 128
N_PAGES = PAST_LEN // PAGE_SIZE
N_META = 16
H_DIFF = 8
HD_DIFF = 64
W_DIFF = H_DIFF * 2 * HD_DIFF
H_SB = 16
HD_SB = 64
W_SB = H_SB * HD_SB
D_FF = 4 * D_MODEL
N_BUCKETS = 32
MAX_DISTANCE = 128
NORM_EPS = 1e-6
NEG_INF = -1e30
LAM_INIT = 0.8 - 0.6 * math.exp(-0.3 * 0)

ROWS_P = BATCH * SEQ
ROWS_S = DEC_BATCH * DEC_SEQ
SMALL = 512
ROWS = ROWS_P + SMALL
META_ROW0 = ROWS_P + ROWS_S
COL = 1024
QSCALE = HD_DIFF ** -0.5

VMEM_LIMIT = 56 * 1024 * 1024


def _bucket_thresholds():
    n = np.arange(0, 4 * MAX_DISTANCE)
    max_exact = N_BUCKETS // 2
    nf = np.maximum(n, 1).astype(np.float64)
    large = max_exact + (np.log(nf / max_exact) / math.log(MAX_DISTANCE / max_exact)
                         * (N_BUCKETS - max_exact)).astype(np.int64)
    bucket = np.where(n < max_exact, n, np.minimum(large, N_BUCKETS - 1))
    return [int(np.argmax(bucket >= b)) for b in range(N_BUCKETS)]


BUCKET_START = _bucket_thresholds()
FAR_DIST = BUCKET_START[-1]


def _params(sem):
    return pltpu.CompilerParams(dimension_semantics=sem, vmem_limit_bytes=VMEM_LIMIT)


TM_NORM = 512
NP_TILES = ROWS_P // TM_NORM


def _rms(x, g):
    ms = jnp.mean(x * x, axis=-1, keepdims=True)
    return x * lax.rsqrt(ms + NORM_EPS) * g


def _norm_kernel(xp_ref, xs_ref, g_ref, o_ref):
    i = pl.program_id(0)

    @pl.when(i < NP_TILES)
    def _():
        o_ref[...] = _rms(xp_ref[...], g_ref[...]).astype(bf16)

    @pl.when(i >= NP_TILES)
    def _():
        o_ref[...] = _rms(xs_ref[...], g_ref[...]).astype(bf16)


def _norm_call(xp, xs, g):
    return pl.pallas_call(
        _norm_kernel,
        grid=(ROWS // TM_NORM,),
        in_specs=[
            pl.BlockSpec((TM_NORM, D_MODEL), lambda i: (jnp.minimum(i, NP_TILES - 1), 0)),
            pl.BlockSpec((TM_NORM, D_MODEL), lambda i: (jnp.maximum(i - NP_TILES, 0), 0)),
            pl.BlockSpec((1, D_MODEL), lambda i: (0, 0)),
        ],
        out_specs=pl.BlockSpec((TM_NORM, D_MODEL), lambda i: (i, 0)),
        out_shape=jax.ShapeDtypeStruct((ROWS, D_MODEL), bf16),
        compiler_params=_params(("arbitrary",)),
        name="norm1",
    )(xp, xs, g)


TM_PROJ = 1088
assert ROWS % TM_PROJ == 0


def _group_sumsq(z, gmat):
    zz = z * z
    hi = zz.astype(bf16)
    lo = (zz - hi.astype(f32)).astype(bf16)
    parts = []
    for c in range(z.shape[1] // 256):
        sl = slice(256 * c, 256 * (c + 1))
        parts.append(jnp.dot(hi[:, sl], gmat, preferred_element_type=f32)
                     + jnp.dot(lo[:, sl], gmat, preferred_element_type=f32))
    return jnp.concatenate(parts, axis=1)


def _qk_norm(z, g, gmat):
    ss = _group_sumsq(z, gmat)
    return z * lax.rsqrt(ss * (1.0 / HD_DIFF) + NORM_EPS) * g


def _proj_q_kernel(x_ref, w_ref, g_ref, gmat_ref, o_ref):
    j = pl.program_id(0)
    z = jnp.dot(x_ref[...], w_ref[...], preferred_element_type=f32)

    @pl.when(j == 0)
    def _():
        o_ref[...] = (_qk_norm(z, g_ref[...], gmat_ref[...]) * QSCALE).astype(bf16)

    @pl.when(j == 1)
    def _():
        o_ref[...] = (z * QSCALE).astype(bf16)


def _proj_kv_kernel(x_ref, w_ref, g_ref, gmat_ref, o32_ref, o16_ref):
    j = pl.program_id(0)
    z = jnp.dot(x_ref[...], w_ref[...], preferred_element_type=f32)

    @pl.when(j == 0)
    def _():
        zn = _qk_norm(z, g_ref[...], gmat_ref[...])
        o32_ref[...] = zn
        o16_ref[...] = zn.astype(bf16)

    @pl.when(j > 0)
    def _():
        o32_ref[...] = z
        o16_ref[...] = z.astype(bf16)


def _proj_gate_kernel(x_ref, w_ref, b_ref, o_ref):
    z = jnp.dot(x_ref[...], w_ref[...], preferred_element_type=f32) + b_ref[...]
    o_ref[...] = (1.0 / (1.0 + jnp.exp(-z))).astype(bf16)


def _proj_calls(xn, w_in, gq, gk, gmat, b_gate):
    n_rt = ROWS // TM_PROJ
    x_spec = pl.BlockSpec((TM_PROJ, D_MODEL), lambda j, i: (i, 0))
    vec_spec = pl.BlockSpec((1, COL), lambda j, i: (0, 0))
    gmat_spec = pl.BlockSpec((256, 256), lambda j, i: (0, 0))
    out_spec = pl.BlockSpec((TM_PROJ, COL), lambda j, i: (i, j))
    sem = ("arbitrary", "arbitrary")

    q = pl.pallas_call(
        _proj_q_kernel,
        grid=(2, n_rt),
        in_specs=[x_spec, pl.BlockSpec((D_MODEL, COL), lambda j, i: (0, 3 * j)), vec_spec, gmat_spec],
        out_specs=out_spec,
        out_shape=jax.ShapeDtypeStruct((ROWS, 2 * COL), bf16),
        compiler_params=_params(sem),
        name="proj_q",
    )(xn, w_in, gq, gmat)

    kv32, kv16 = pl.pallas_call(
        _proj_kv_kernel,
        grid=(4, n_rt),
        in_specs=[x_spec, pl.BlockSpec((D_MODEL, COL), lambda j, i: (0, j + 1 + j // 2)), vec_spec, gmat_spec],
        out_specs=[out_spec, out_spec],
        out_shape=[jax.ShapeDtypeStruct((ROWS, 4 * COL), f32), jax.ShapeDtypeStruct((ROWS, 4 * COL), bf16)],
        compiler_params=_params(sem),
        name="proj_kv",
    )(xn, w_in, gk, gmat)

    gates = pl.pallas_call(
        _proj_gate_kernel,
        grid=(4, n_rt),
        in_specs=[x_spec, pl.BlockSpec((D_MODEL, COL), lambda j, i: (0, j + 6)),
                  pl.BlockSpec((1, COL), lambda j, i: (0, j))],
        out_specs=out_spec,
        out_shape=jax.ShapeDtypeStruct((ROWS, 4 * COL), bf16),
        compiler_params=_params(sem),
        name="proj_gate",
    )(xn, w_in, b_gate)
    return q, kv32, kv16, gates


def _nt_dot(a, b):
    return lax.dot_general(a, b, (((1,), (1,)), ((), ())), preferred_element_type=f32)


def _diff_update(parts, m_ref, l_ref, acc_ref, key_axis):
    m_old = m_ref[...]
    m_new = m_old
    for s, _ in parts:
        m_new = jnp.maximum(m_new, jnp.max(s, axis=key_axis, keepdims=True))
    alpha = jnp.exp(m_old - m_new)
    l_new = alpha * l_ref[...]
    acc = alpha * acc_ref[...]
    for s, pv_fn in parts:
        p = jnp.exp(s - m_new)
        l_new = l_new + jnp.sum(p, axis=key_axis, keepdims=True)
        acc = acc + pv_fn(p.astype(bf16))
    l_ref[...] = l_new
    acc_ref[...] = acc
    m_ref[...] = m_new


SIGN_BIT = -2 ** 31
SB_CUTOFF = 105.0


def _sb_update(parts, later_fn, c_ref, acc_ref, key_axis):
    pre = []
    for z, mask, _ in parts:
        neg_abs = lax.bitcast_convert_type(lax.bitcast_convert_type(z, jnp.int32) | SIGN_BIT, f32)
        sp = jnp.maximum(z, 0.0) + jnp.log(1.0 + jnp.exp(neg_abs))
        base = z - sp
        if mask is not None:
            sp = jnp.where(mask, sp, 0.0)
        hi = sp.astype(bf16)
        lo = (sp - hi.astype(f32)).astype(bf16)
        pre.append((base - (later_fn(hi) + later_fn(lo)), jnp.sum(sp, axis=key_axis, keepdims=True)))
    c = c_ref[...]
    acc = acc_ref[...]
    for (base, tot), (_, mask, pv_fn) in zip(pre, parts):
        w = jnp.exp(base - c)
        if mask is not None:
            w = jnp.where(mask, w, 0.0)
        acc = acc + pv_fn(w.astype(bf16))
        c = c + tot
    c_ref[...] = c
    acc_ref[...] = acc


def _bias_chain(dist, rb_of):
    b = jnp.where(dist >= BUCKET_START[1], rb_of(1), rb_of(0))
    for k in range(2, N_BUCKETS):
        b = jnp.where(dist >= BUCKET_START[k], rb_of(k), b)
    return b


def _lam(lq1, lk1, lq2, lk2):
    return (jnp.exp(jnp.sum(lq1 * lk1, axis=1, keepdims=True))
            - jnp.exp(jnp.sum(lq2 * lk2, axis=1, keepdims=True)) + LAM_INIT)


def _sub_norm(od, sg):
    ms = jnp.mean(od * od, axis=-1, keepdims=True)
    return (od * lax.rsqrt(ms + NORM_EPS) * sg) * (1.0 - LAM_INIT)


def _nn(v):
    return lambda p: jnp.dot(p, v, preferred_element_type=f32)


TQ = 256
NQ = SEQ // TQ
META_TILE = 128
assert TQ + 1 >= FAR_DIST and N_META + TQ - (N_META - 1) >= FAR_DIST


def _left(vt):
    return lambda p: jnp.dot(vt, p, preferred_element_type=f32)


def _transpose_bf16(x):
    return x.astype(f32).T.astype(bf16)


def _attn_kernel(rb_ref, qd_ref, qs_ref, kd_ref, vd_ref, ks_ref, vs_ref,
                 kdm_ref, vdm_ref, ksm_ref, vsm_ref,
                 lq1_ref, lk1_ref, lq2_ref, lk2_ref, sg_ref, a_ref, am_ref,
                 od_ref, os_ref,
                 b0_ref, b1_ref, bm_ref, vdt_ref, vst_ref, vdmt_ref, vsmt_ref,
                 md_ref, ld_ref, accd_ref, cs_ref, accs_ref):
    h = pl.program_id(0)
    qi = pl.program_id(2)
    R = 2 * TQ

    def rb_of(k):
        return rb_ref[k, h]

    key = lax.broadcasted_iota(jnp.int32, (TQ, R), 0)
    qry = lax.broadcasted_iota(jnp.int32, (TQ, R), 1) % TQ

    far_bias = rb_of(N_BUCKETS - 1)

    @pl.when(jnp.logical_and(pl.program_id(1) == 0, qi == 0))
    def _():
        b0_ref[...] = _bias_chain(qry - key, rb_of) - far_bias
        b1_ref[...] = _bias_chain(TQ + qry - key, rb_of) - far_bias
        mkey = lax.broadcasted_iota(jnp.int32, (N_META, R), 0)
        mqry = lax.broadcasted_iota(jnp.int32, (N_META, R), 1) % TQ
        bm_ref[...] = _bias_chain(N_META + mqry - mkey, rb_of) - far_bias

    @pl.when(qi == 0)
    def _():
        for j in range(NQ):
            vdt_ref[j] = _transpose_bf16(vd_ref[j * TQ:(j + 1) * TQ, :])
            vst_ref[j] = _transpose_bf16(vs_ref[j * TQ:(j + 1) * TQ, :])
        vdmt_ref[...] = _transpose_bf16(vdm_ref[...])
        vsmt_ref[...] = _transpose_bf16(vsm_ref[...])

    md_ref[...] = jnp.full((1, R), NEG_INF, f32)
    ld_ref[...] = jnp.zeros((1, R), f32)
    accd_ref[...] = jnp.zeros((128, R), f32)
    cs_ref[...] = jnp.zeros((1, R), f32)
    accs_ref[...] = jnp.zeros((128, R), f32)

    lane = lax.broadcasted_iota(jnp.int32, (1, 128), 1)
    lo_half = (lane < HD_DIFF).astype(f32).astype(bf16)
    hi_half = (lane >= HD_DIFF).astype(f32).astype(bf16)
    qd = qd_ref[...]
    qs = qs_ref[...]
    qd2 = jnp.concatenate([qd * lo_half, qd * hi_half], axis=0)
    qs2 = jnp.concatenate([qs * lo_half, qs * hi_half], axis=0)
    a = a_ref[...]

    def later(x):
        return jnp.dot(a, x, preferred_element_type=f32)

    def tiles(specs, with_sb, extra_diff=()):
        dparts = list(extra_diff)
        sparts = []
        for j, bias, dmask, smask in specs:
            sl = pl.ds(pl.multiple_of(j * TQ, TQ), TQ)
            s = _nt_dot(kd_ref[sl, :], qd2)
            if bias is not None:
                s = s + bias
            if dmask is not None:
                s = jnp.where(dmask, s, NEG_INF)
            dparts.append((s, _left(vdt_ref[j])))
            if with_sb:
                sparts.append((_nt_dot(ks_ref[sl, :], qs2), smask, _left(vst_ref[j])))
        _diff_update(dparts, md_ref, ld_ref, accd_ref, 0)
        if with_sb:
            _sb_update(sparts, later, cs_ref, accs_ref, 0)

    def far(js, with_sb):
        tiles([(j, None, None, None) for j in js], with_sb)

    meta_bias = bm_ref[...] * jnp.where(qi == 0, 1.0, 0.0)
    meta_diff = (_nt_dot(kdm_ref[0:N_META, :], qd2) + meta_bias, _left(vdmt_ref[:, 0:N_META]))
    diag = (qi, b0_ref[...], qry >= key, qry > key)

    @pl.when(qi == 0)
    def _():
        tiles([diag], True, [meta_diff])

    @pl.when(qi >= 1)
    def _():
        tiles([diag, (qi - 1, b1_ref[...], None, None)], True, [meta_diff])

    sb_live = jnp.min(cs_ref[...], axis=1, keepdims=True)[0, 0] < SB_CUTOFF
    n_far = jnp.maximum(qi - 1, 0)

    @pl.when(sb_live)
    def _():
        def pair(t, carry):
            j = qi - 2 - 2 * t
            far([j, j - 1], True)
            return carry

        lax.fori_loop(0, n_far // 2, pair, 0)

        @pl.when(n_far % 2 == 1)
        def _():
            far([0], True)

        am = am_ref[...]
        _sb_update([(_nt_dot(ksm_ref[0:N_META, :], qs2), None, _left(vsmt_ref[:, 0:N_META]))],
                   lambda x: jnp.dot(am, x, preferred_element_type=f32), cs_ref, accs_ref, 0)

    @pl.when(jnp.logical_not(sb_live))
    def _():
        def quad(t, carry):
            j = qi - 2 - 4 * t
            far([j, j - 1, j - 2, j - 3], False)
            return carry

        lax.fori_loop(0, n_far // 4, quad, 0)

        def single(t, carry):
            far([n_far % 4 - 1 - t], False)
            return carry

        lax.fori_loop(0, n_far % 4, single, 0)

    lam = _lam(lq1_ref[...], lk1_ref[...], lq2_ref[...], lk2_ref[...])
    o = accd_ref[...] / ld_ref[...]
    odt = o[:, :TQ] - lam * o[:, TQ:]
    od_ref[...] = _sub_norm(odt.T, sg_ref[...]).astype(bf16)
    accs = accs_ref[...]
    vrow = lax.broadcasted_iota(jnp.int32, (128, TQ), 0)
    os_ref[...] = jnp.where(vrow < HD_SB, accs[:, :TQ], accs[:, TQ:]).T.astype(bf16)


def _attn_call(rel_bias, q, kv16, lq1, lk1, lq2, lk2, sg, a, am):
    R = 2 * TQ
    nh = H_DIFF
    meta_blk = META_ROW0 // META_TILE
    assert META_ROW0 % META_TILE == 0

    def qspec(off):
        return pl.BlockSpec((TQ, 128), lambda h, b, i: (b * NQ + i, off + h))

    def kvspec(off):
        return pl.BlockSpec((SEQ, 128), lambda h, b, i: (b, off + h))

    def mspec(off):
        return pl.BlockSpec((META_TILE, 128), lambda h, b, i: (meta_blk, off + h))

    def const(shape):
        return pl.BlockSpec(shape, lambda h, b, i: (0, 0))

    out_spec = pl.BlockSpec((TQ, 128), lambda h, b, i: (b * NQ + i, h))
    return pl.pallas_call(
        _attn_kernel,
        grid=(nh, BATCH, NQ),
        in_specs=[pl.BlockSpec(memory_space=pltpu.SMEM),
                  qspec(0), qspec(nh),
                  kvspec(0), kvspec(nh), kvspec(2 * nh), kvspec(3 * nh),
                  mspec(0), mspec(nh), mspec(2 * nh), mspec(3 * nh),
                  const((1, HD_DIFF)), const((1, HD_DIFF)), const((1, HD_DIFF)), const((1, HD_DIFF)),
                  const((1, 128)), const((TQ, TQ)), const((N_META, N_META))],
        out_specs=[out_spec, out_spec],
        out_shape=[jax.ShapeDtypeStruct((ROWS_P, W_DIFF), bf16), jax.ShapeDtypeStruct((ROWS_P, W_SB), bf16)],
        scratch_shapes=[pltpu.VMEM((TQ, R), f32), pltpu.VMEM((TQ, R), f32), pltpu.VMEM((N_META, R), f32),
                        pltpu.VMEM((NQ, 128, TQ), bf16), pltpu.VMEM((NQ, 128, TQ), bf16),
                        pltpu.VMEM((128, META_TILE), bf16), pltpu.VMEM((128, META_TILE), bf16),
                        pltpu.VMEM((1, R), f32), pltpu.VMEM((1, R), f32), pltpu.VMEM((128, R), f32),
                        pltpu.VMEM((1, R), f32), pltpu.VMEM((128, R), f32)],
        compiler_params=_params(("arbitrary", "arbitrary", "arbitrary")),
        name="attn_prompt",
    )(rel_bias, q, q, kv16, kv16, kv16, kv16, kv16, kv16, kv16, kv16, lq1, lk1, lq2, lk2, sg, a, am)


PPS = 8
PGROUP = 4
NSTEP = N_PAGES // PPS
GPS = PPS // PGROUP
assert N_PAGES % PPS == 0 and PPS % PGROUP == 0 and GPS % 2 == 0
RS = 128
PH = PAGE_SIZE * H_DIFF
assert H_DIFF * 2 * DEC_SEQ == RS and H_SB * DEC_SEQ == RS
assert PAGE_SIZE + 1 >= FAR_DIST


def _diag_blocks_sb(pv):
    return jnp.concatenate([pv[8 * h:8 * (h + 1), 128 * (h // 2):128 * (h // 2 + 1)] for h in range(H_SB)], axis=0)


def _sample_kernel(pt_ref, rbrow_ref, qx_ref, qs_ref, kdn_ref, vdn_ref, ksn_ref, vsn_ref, *rest):
    diff_refs = rest[:2 * PPS]
    (ksb_hbm, vsb_hbm, lq1_ref, lk1_ref, lq2_ref, lk2_ref, sg_ref, u_ref, od_ref, os_ref,
     alast_ref, afar_ref, md_ref, ld_ref, accd_ref, cs_ref, accs_ref,
     ksbuf, vsbuf, sem, live_ref) = rest[2 * PPS:]
    b = pl.program_id(0)
    s = pl.program_id(1)
    qx = qx_ref[...]
    qs = qs_ref[...]

    def sb_copies(bb, group, slot):
        cps = []
        for k in range(PGROUP):
            page = pt_ref[bb, N_PAGES - 1 - (group * PGROUP + k)]
            cps.append(pltpu.make_async_copy(ksb_hbm.at[page], ksbuf.at[slot, k], sem.at[slot]))
            cps.append(pltpu.make_async_copy(vsb_hbm.at[page], vsbuf.at[slot, k], sem.at[slot]))
        return cps

    @pl.when(jnp.logical_and(b == 0, s == 0))
    def _():
        for cp in sb_copies(0, 0, 0):
            cp.start()
        live_ref[0] = 1

    def rb_col(k):
        return rbrow_ref[:, k:k + 1]

    def diff_geometry(n_lanes, base):
        r = lax.broadcasted_iota(jnp.int32, (RS, n_lanes), 0)
        ln = lax.broadcasted_iota(jnp.int32, (RS, n_lanes), 1)
        same_head = (ln % H_DIFF) == (r // (2 * DEC_SEQ))
        dist = base + (r % DEC_SEQ) - (ln // H_DIFF)
        return same_head, dist

    def diff_part(kd, vd, add):
        return (_nt_dot(qx, kd.astype(bf16)) + add, _nn(vd.astype(bf16)))

    u = u_ref[...]

    def later(x):
        return jnp.dot(x, u, preferred_element_type=f32)

    far_bias = rb_col(N_BUCKETS - 1)

    @pl.when(s == 0)
    def _():
        md_ref[...] = jnp.full((RS, 1), NEG_INF, f32)
        ld_ref[...] = jnp.zeros((RS, 1), f32)
        accd_ref[...] = jnp.zeros((RS, 128), f32)
        cs_ref[...] = jnp.zeros((RS, 1), f32)
        accs_ref[...] = jnp.zeros((RS, 128), f32)
        same, dist = diff_geometry(DEC_SEQ * H_DIFF, 0)
        add = jnp.where(same & (dist >= 0), _bias_chain(dist, rb_col) - far_bias, NEG_INF)
        _diff_update([diff_part(kdn_ref[...], vdn_ref[...], add)], md_ref, ld_ref, accd_ref, 1)
        zpad = jnp.zeros((PAGE_SIZE - DEC_SEQ, COL), f32)
        ksn = jnp.concatenate([ksn_ref[...], zpad], axis=0).astype(bf16)
        vsn = jnp.concatenate([vsn_ref[...], zpad], axis=0).astype(bf16)
        qidx = lax.broadcasted_iota(jnp.int32, (RS, PAGE_SIZE), 0) % DEC_SEQ
        key = lax.broadcasted_iota(jnp.int32, (RS, PAGE_SIZE), 1)
        smask = (qidx > key) & (key < DEC_SEQ)
        _sb_update([(_nt_dot(qs, ksn), smask,
                     lambda w: _diag_blocks_sb(jnp.dot(w, vsn, preferred_element_type=f32)))],
                   later, cs_ref, accs_ref, 1)
        same, dist = diff_geometry(PH, PAGE_SIZE)
        alast_ref[...] = jnp.where(same, _bias_chain(dist, rb_col) - far_bias, NEG_INF)
        afar_ref[...] = jnp.where(same, 0.0, NEG_INF)

    @pl.when(s == 1)
    def _():
        alast_ref[...] = afar_ref[...]

    dparts = []
    for k in range(PPS):
        kd_ref, vd_ref = diff_refs[2 * k:2 * k + 2]
        dparts.append(diff_part(kd_ref[...], vd_ref[...], alast_ref[...] if k == 0 else afar_ref[...]))
    _diff_update(dparts, md_ref, ld_ref, accd_ref, 1)

    for g in range(GPS):
        group = s * GPS + g
        slot = g % 2
        live = live_ref[0] == 1

        @pl.when(live)
        def _():
            for cp in sb_copies(b, group, slot):
                cp.wait()
            sparts = []
            for k in range(PGROUP):
                vt = vsbuf[slot, k].astype(bf16)
                sparts.append((jnp.dot(qs, ksbuf[slot, k].astype(bf16), preferred_element_type=f32), None,
                               lambda w, vt=vt: _diag_blocks_sb(_nt_dot(w, vt))))
            _sb_update(sparts, later, cs_ref, accs_ref, 1)

        still_live = jnp.logical_and(live, jnp.min(cs_ref[...], axis=0, keepdims=True)[0, 0] < SB_CUTOFF)
        if g < GPS - 1:
            next_same_seq = still_live
            next_seq = False
        else:
            last = s == NSTEP - 1
            next_same_seq = jnp.logical_and(jnp.logical_not(last), still_live)
            next_seq = jnp.logical_and(last, b + 1 < DEC_BATCH)

            @pl.when(next_seq)
            def _():
                for cp in sb_copies(b + 1, 0, 0):
                    cp.start()

        @pl.when(next_same_seq)
        def _():
            for cp in sb_copies(b, group + 1, (g + 1) % 2):
                cp.start()

        live_ref[0] = jnp.logical_or(next_same_seq, next_seq).astype(jnp.int32)

    @pl.when(s == NSTEP - 1)
    def _():
        lam = _lam(lq1_ref[...], lk1_ref[...], lq2_ref[...], lk2_ref[...])
        o = accd_ref[...] / ld_ref[...]
        sg = sg_ref[...]
        heads = []
        for h in range(H_DIFF):
            od = o[16 * h:16 * h + 8] - lam * o[16 * h + 8:16 * h + 16]
            heads.append(_sub_norm(od, sg))
        od_ref[...] = jnp.concatenate(heads, axis=1).astype(bf16)
        accs = accs_ref[...]
        lane = lax.broadcasted_iota(jnp.int32, (1, 128), 1)
        pairs = [jnp.where(lane < HD_SB, accs[16 * t:16 * t + 8], accs[16 * t + 8:16 * t + 16])
                 for t in range(H_SB // 2)]
        os_ref[...] = jnp.concatenate(pairs, axis=1).astype(bf16)


def _sample_call(page_table, rbrow, qx, qbd_s, kdn, vdn, ksn, vsn, caches, lq1, lk1, lq2, lk2, sg, u):
    def const2(shape):
        return pl.BlockSpec(shape, lambda b, s, pt: (0, 0))

    def per_b(shape):
        return pl.BlockSpec((None,) + shape, lambda b, s, pt: (b, 0, 0))

    def cache_spec(k):
        return pl.BlockSpec((None, COL, PAGE_SIZE),
                            lambda b, s, pt: (pt[b, N_PAGES - 1 - (s * PPS + k)], 0, 0))

    in_specs = [const2((RS, N_BUCKETS)), per_b((RS, 128)), per_b((RS, COL)),
                per_b((DEC_SEQ * H_DIFF, 128)), per_b((DEC_SEQ * H_DIFF, 128)),
                per_b((DEC_SEQ, COL)), per_b((DEC_SEQ, COL))]
    args = [rbrow, qx, qbd_s, kdn, vdn, ksn, vsn]
    kd_cache, vd_cache, ks_cache, vs_cache = caches
    for k in range(PPS):
        for c in (kd_cache, vd_cache):
            in_specs.append(cache_spec(k))
            args.append(c)
    in_specs += [pl.BlockSpec(memory_space=pl.ANY)] * 2
    args += [ks_cache, vs_cache]
    in_specs += [const2((1, HD_DIFF))] * 4 + [const2((1, 128)), const2((PAGE_SIZE, PAGE_SIZE))]
    args += [lq1, lk1, lq2, lk2, sg, u]
    out_spec = per_b((DEC_SEQ, COL))
    grid_spec = pltpu.PrefetchScalarGridSpec(
        num_scalar_prefetch=1,
        grid=(DEC_BATCH, NSTEP),
        in_specs=in_specs,
        out_specs=[out_spec, out_spec],
        scratch_shapes=[pltpu.VMEM((RS, PH), f32), pltpu.VMEM((RS, PH), f32),
                        pltpu.VMEM((RS, 1), f32), pltpu.VMEM((RS, 1), f32), pltpu.VMEM((RS, 128), f32),
                        pltpu.VMEM((RS, 1), f32), pltpu.VMEM((RS, 128), f32),
                        pltpu.VMEM((2, PGROUP, COL, PAGE_SIZE), f32), pltpu.VMEM((2, PGROUP, COL, PAGE_SIZE), f32),
                        pltpu.SemaphoreType.DMA((2,)), pltpu.SMEM((1,), jnp.int32)],
    )
    return pl.pallas_call(
        _sample_kernel,
        grid_spec=grid_spec,
        out_shape=[jax.ShapeDtypeStruct((DEC_BATCH, DEC_SEQ, W_DIFF), bf16),
                   jax.ShapeDtypeStruct((DEC_BATCH, DEC_SEQ, W_SB), bf16)],
        compiler_params=_params(("arbitrary", "arbitrary")),
        name="attn_sample",
    )(page_table, *args)


TM_MIX = 256
NP_MIX = ROWS_P // TM_MIX


def _mix_kernel(odp_ref, osp_ref, ods_ref, oss_ref, g_ref, xp_ref, xs_ref,
                wa_ref, wb_ref, wo_ref, n2_ref, h2_ref, xn2_ref):
    i = pl.program_id(0)

    def body(od, osb, x):
        ya = jnp.dot(od, wa_ref[...], preferred_element_type=f32)
        yb = jnp.dot(osb, wb_ref[...], preferred_element_type=f32)
        g = g_ref[...]
        mix = g[:, :D_MODEL].astype(f32) * ya + g[:, D_MODEL:].astype(f32) * yb
        h2 = x + jnp.dot(mix.astype(bf16), wo_ref[...], preferred_element_type=f32)
        h2_ref[...] = h2
        xn2_ref[...] = _rms(h2, n2_ref[...]).astype(bf16)

    @pl.when(i < NP_MIX)
    def _():
        body(odp_ref[...], osp_ref[...], xp_ref[...])

    @pl.when(i >= NP_MIX)
    def _():
        body(ods_ref[...], oss_ref[...], xs_ref[...])


def _mix_call(od_p, os_p, od_s, os_s, gates, xp, xs, wa, wb, wo, n2):
    def prow(i):
        return (jnp.minimum(i, NP_MIX - 1), 0)

    def srow(i):
        return (jnp.maximum(i - NP_MIX, 0), 0)

    def full(shape):
        return pl.BlockSpec(shape, lambda i: (0, 0), pipeline_mode=pl.Buffered(1))

    return pl.pallas_call(
        _mix_kernel,
        grid=(ROWS // TM_MIX,),
        in_specs=[pl.BlockSpec((TM_MIX, W_DIFF), prow), pl.BlockSpec((TM_MIX, W_SB), prow),
                  pl.BlockSpec((TM_MIX, W_DIFF), srow), pl.BlockSpec((TM_MIX, W_SB), srow),
                  pl.BlockSpec((TM_MIX, 2 * D_MODEL), lambda i: (i, 0)),
                  pl.BlockSpec((TM_MIX, D_MODEL), prow), pl.BlockSpec((TM_MIX, D_MODEL), srow),
                  full((W_DIFF, D_MODEL)), full((W_SB, D_MODEL)), full((D_MODEL, D_MODEL)),
                  full((1, D_MODEL))],
        out_specs=[pl.BlockSpec((TM_MIX, D_MODEL), lambda i: (i, 0)),
                   pl.BlockSpec((TM_MIX, D_MODEL), lambda i: (i, 0))],
        out_shape=[jax.ShapeDtypeStruct((ROWS, D_MODEL), f32), jax.ShapeDtypeStruct((ROWS, D_MODEL), bf16)],
        compiler_params=_params(("arbitrary",)),
        name="mix_out",
    )(od_p, os_p, od_s, os_s, gates, xp, xs, wa, wb, wo, n2)


TM_MLP = 512
TF_MLP = 1024


def _mlp_kernel(xn_ref, h2_ref, wu_ref, wd_ref, o_ref):
    f = pl.program_id(1)
    u = jnp.maximum(jnp.dot(xn_ref[...], wu_ref[...], preferred_element_type=f32), 0.0)
    part = jnp.dot((u * u).astype(bf16), wd_ref[...], preferred_element_type=f32)

    @pl.when(f == 0)
    def _():
        o_ref[...] = h2_ref[...] + part

    @pl.when(f > 0)
    def _():
        o_ref[...] += part


def _mlp_call(xn2, h2, wu, wd):
    return pl.pallas_call(
        _mlp_kernel,
        grid=(ROWS // TM_MLP, D_FF // TF_MLP),
        in_specs=[pl.BlockSpec((TM_MLP, D_MODEL), lambda i, f: (i, 0)),
                  pl.BlockSpec((TM_MLP, D_MODEL), lambda i, f: (i, 0)),
                  pl.BlockSpec((D_MODEL, TF_MLP), lambda i, f: (0, f)),
                  pl.BlockSpec((TF_MLP, D_MODEL), lambda i, f: (f, 0))],
        out_specs=pl.BlockSpec((TM_MLP, D_MODEL), lambda i, f: (i, 0)),
        out_shape=jax.ShapeDtypeStruct((ROWS, D_MODEL), f32),
        compiler_params=_params(("arbitrary", "arbitrary")),
        name="mlp",
    )(xn2, h2, wu, wd)


def _later_matrix(n, keys_on_rows=False):
    r = np.arange(n)
    m = r[:, None] > r[None, :]
    return jnp.asarray((m.T if keys_on_rows else m).astype(np.float32), dtype=bf16)


def _block_diag_queries(qrows, n_groups):
    gw = COL // n_groups
    grp = jnp.arange(COL, dtype=jnp.int32) // gw
    keep = grp[None, :] == jnp.arange(n_groups, dtype=jnp.int32)[:, None]
    out = jnp.where(keep[None, :, None, :], qrows[:, None, :, :], jnp.zeros((), qrows.dtype))
    return out.reshape(DEC_BATCH, n_groups * DEC_SEQ, COL)


def kernel(x_prompt, x_sample, cache_k_diff, cache_v_diff, cache_k_sb, cache_v_sb, page_table, meta_tokens,
           rel_bias, norm1_g, w_in, b_gate, qk_norm_q, qk_norm_k, lam_q1, lam_k1, lam_q2, lam_k2, subln_g,
           w_branch_a, w_branch_b, w_o, norm2_g, w_up, w_down):
    l = 0
    xp = x_prompt.reshape(ROWS_P, D_MODEL)
    xs = jnp.concatenate([x_sample.reshape(ROWS_S, D_MODEL), meta_tokens.astype(f32),
                          jnp.zeros((SMALL - ROWS_S - N_META, D_MODEL), f32)], axis=0)
    w_in16 = w_in[l].astype(bf16)
    wa16 = w_branch_a[l].astype(bf16)
    wb16 = w_branch_b[l].astype(bf16)
    wo16 = w_o[l].astype(bf16)
    wu16 = w_up[l].astype(bf16)
    wd16 = w_down[l].astype(bf16)
    gq = jnp.tile(qk_norm_q[l].astype(f32), COL // HD_DIFF)[None]
    gk = jnp.tile(qk_norm_k[l].astype(f32), COL // HD_DIFF)[None]
    grp = np.arange(256) // HD_DIFF
    gmat = jnp.asarray((grp[:, None] == grp[None, :]).astype(np.float32), dtype=bf16)
    lq1, lk1, lq2, lk2 = (a[l].astype(f32)[None] for a in (lam_q1, lam_k1, lam_q2, lam_k2))
    sg = subln_g[l].astype(f32)[None]
    rb = rel_bias.astype(f32)

    xn = _norm_call(xp, xs, norm1_g[l].astype(f32)[None])
    q, kv32, kv16, gates = _proj_calls(xn, w_in16, gq, gk, gmat, b_gate[l].astype(f32)[None])

    od_p, os_p = _attn_call(rb, q, kv16, lq1, lk1, lq2, lk2, sg,
                            _later_matrix(TQ, keys_on_rows=True), _later_matrix(N_META, keys_on_rows=True))

    q_s = q[ROWS_P:ROWS_P + ROWS_S].reshape(DEC_BATCH, DEC_SEQ, 2 * COL)
    qd_s = jnp.transpose(q_s[..., :COL].reshape(DEC_BATCH, DEC_SEQ, H_DIFF, 2 * HD_DIFF), (0, 2, 1, 3))
    half = (jnp.arange(2 * HD_DIFF, dtype=jnp.int32) // HD_DIFF)[None, :] == jnp.arange(2, dtype=jnp.int32)[:, None]
    qx = jnp.where(half[None, None, :, None, :], qd_s[:, :, None], jnp.zeros((), bf16)).reshape(DEC_BATCH, RS, 128)
    qbd_s = _block_diag_queries(q_s[..., COL:], H_SB)
    new32 = kv32[ROWS_P:ROWS_P + ROWS_S].reshape(DEC_BATCH, DEC_SEQ, 4 * COL)
    kdn = new32[..., 0:COL].reshape(DEC_BATCH, DEC_SEQ * H_DIFF, 2 * HD_DIFF)
    vdn = new32[..., COL:2 * COL].reshape(DEC_BATCH, DEC_SEQ * H_DIFF, 2 * HD_DIFF)
    ksn = new32[..., 2 * COL:3 * COL]
    vsn = new32[..., 3 * COL:4 * COL]
    n_phys = cache_k_diff.shape[1]
    caches = [cache_k_diff[l].reshape(n_phys, COL, 2 * HD_DIFF), cache_v_diff[l].reshape(n_phys, COL, 2 * HD_DIFF),
              jnp.transpose(cache_k_sb[l], (0, 2, 3, 1)).reshape(n_phys, COL, PAGE_SIZE),
              jnp.transpose(cache_v_sb[l], (0, 2, 3, 1)).reshape(n_phys, COL, PAGE_SIZE)]
    rbrow = jnp.repeat(rb.T, 2 * DEC_SEQ, axis=0)
    od_s, os_s = _sample_call(page_table, rbrow, qx, qbd_s, kdn, vdn, ksn, vsn, caches, lq1, lk1, lq2, lk2, sg,
                              _later_matrix(PAGE_SIZE))
    pad = jnp.zeros((SMALL - ROWS_S, COL), bf16)
    od_s = jnp.concatenate([od_s.reshape(ROWS_S, COL), pad], axis=0)
    os_s = jnp.concatenate([os_s.reshape(ROWS_S, COL), pad], axis=0)

    h2, xn2 = _mix_call(od_p, os_p, od_s, os_s, gates, xp, xs, wa16, wb16, wo16, norm2_g[l].astype(f32)[None])
    y = _mlp_call(xn2, h2, wu16, wd16)

    y_prompt = y[:ROWS_P].reshape(BATCH, SEQ, D_MODEL)
    y_sample = y[ROWS_P:ROWS_P + ROWS_S].reshape(DEC_BATCH, DEC_SEQ, D_MODEL)

    def prompt_kv(c, heads, width):
        body = kv32[:ROWS_P, c * COL:(c + 1) * COL].reshape(BATCH, SEQ, heads, width)
        meta = kv32[META_ROW0:META_ROW0 + N_META, c * COL:(c + 1) * COL].reshape(1, N_META, heads, width)
        return jnp.concatenate([jnp.broadcast_to(meta, (BATCH, N_META, heads, width)), body], axis=1)[None]

    def sample_kv(c, heads, width):
        return kv32[ROWS_P:ROWS_P + ROWS_S, c * COL:(c + 1) * COL].reshape(1, DEC_BATCH, DEC_SEQ, heads, width)

    return (y_prompt, y_sample,
            prompt_kv(0, H_DIFF, 2 * HD_DIFF), prompt_kv(1, H_DIFF, 2 * HD_DIFF),
            prompt_kv(2, H_SB, HD_SB), prompt_kv(3, H_SB, HD_SB),
            sample_kv(0, H_DIFF, 2 * HD_DIFF), sample_kv(1, H_DIFF, 2 * HD_DIFF),
            sample_kv(2, H_SB, HD_SB), sample_kv(3, H_SB, HD_SB))
```

```python
import functools
import math

import numpy as np
import jax
import jax.numpy as jnp
from jax import lax
from jax.experimental import pallas as pl
from jax.experimental.pallas import tpu as pltpu

f32 = jnp.float32
bf16 = jnp.bfloat16

D_MODEL = 2048
BATCH = 4
SEQ = 2048
DEC_BATCH = 32
DEC_SEQ = 8
PAST_LEN = 8192
PAGE_SIZE = 128
N_PAGES = PAST_LEN // PAGE_SIZE
N_META = 16
H_DIFF = 8
HD_DIFF = 64
W_DIFF = H_DIFF * 2 * HD_DIFF
H_SB = 16
HD_SB = 64
W_SB = H_SB * HD_SB
D_FF = 4 * D_MODEL
N_BUCKETS = 32
MAX_DISTANCE = 128
NORM_EPS = 1e-6
NEG_INF = -1e30
LAM_INIT = 0.8 - 0.6 * math.exp(-0.3 * 0)

ROWS_P = BATCH * SEQ
ROWS_S = DEC_BATCH * DEC_SEQ
SMALL = 512
ROWS = ROWS_P + SMALL
META_ROW0 = ROWS_P + ROWS_S
COL = 1024
QSCALE = HD_DIFF ** -0.5

VMEM_LIMIT = 56 * 1024 * 1024


def _bucket_thresholds():
    n = np.arange(0, 4 * MAX_DISTANCE)
    max_exact = N_BUCKETS // 2
    nf = np.maximum(n, 1).astype(np.float64)
    large = max_exact + (np.log(nf / max_exact) / math.log(MAX_DISTANCE / max_exact)
                         * (N_BUCKETS - max_exact)).astype(np.int64)
    bucket = np.where(n < max_exact, n, np.minimum(large, N_BUCKETS - 1))
    return [int(np.argmax(bucket >= b)) for b in range(N_BUCKETS)]


BUCKET_START = _bucket_thresholds()
FAR_DIST = BUCKET_START[-1]


def _params(sem):
    return pltpu.CompilerParams(dimension_semantics=sem, vmem_limit_bytes=VMEM_LIMIT)


TM_NORM = 512
NP_TILES = ROWS_P // TM_NORM


def _rms(x, g):
    ms = jnp.mean(x * x, axis=-1, keepdims=True)
    return x * lax.rsqrt(ms + NORM_EPS) * g


def _norm_kernel(xp_ref, xs_ref, g_ref, o_ref):
    i = pl.program_id(0)

    @pl.when(i < NP_TILES)
    def _():
        o_ref[...] = _rms(xp_ref[...], g_ref[...]).astype(bf16)

    @pl.when(i >= NP_TILES)
    def _():
        o_ref[...] = _rms(xs_ref[...], g_ref[...]).astype(bf16)


def _norm_call(xp, xs, g):
    return pl.pallas_call(
        _norm_kernel,
        grid=(ROWS // TM_NORM,),
        in_specs=[
            pl.BlockSpec((TM_NORM, D_MODEL), lambda i: (jnp.minimum(i, NP_TILES - 1), 0)),
            pl.BlockSpec((TM_NORM, D_MODEL), lambda i: (jnp.maximum(i - NP_TILES, 0), 0)),
            pl.BlockSpec((1, D_MODEL), lambda i: (0, 0)),
        ],
        out_specs=pl.BlockSpec((TM_NORM, D_MODEL), lambda i: (i, 0)),
        out_shape=jax.ShapeDtypeStruct((ROWS, D_MODEL), bf16),
        compiler_params=_params(("arbitrary",)),
        name="norm1",
    )(xp, xs, g)


TM_PROJ = 1088
TM_KVS = 512
assert ROWS % TM_PROJ == 0 and SEQ % TM_KVS == 0 and ROWS_P % SMALL == 0


def _group_sumsq(z, gmat):
    zz = z * z
    hi = zz.astype(bf16)
    lo = (zz - hi.astype(f32)).astype(bf16)
    parts = []
    for c in range(z.shape[1] // 256):
        sl = slice(256 * c, 256 * (c + 1))
        parts.append(jnp.dot(hi[:, sl], gmat, preferred_element_type=f32)
                     + jnp.dot(lo[:, sl], gmat, preferred_element_type=f32))
    return jnp.concatenate(parts, axis=1)


def _qk_norm(z, g, gmat):
    ss = _group_sumsq(z, gmat)
    return z * lax.rsqrt(ss * (1.0 / HD_DIFF) + NORM_EPS) * g


def _proj_q_kernel(x_ref, w_ref, g_ref, gmat_ref, o_ref):
    j = pl.program_id(0)
    z = jnp.dot(x_ref[...], w_ref[...], preferred_element_type=f32)

    @pl.when(j == 0)
    def _():
        o_ref[...] = (_qk_norm(z, g_ref[...], gmat_ref[...]) * QSCALE).astype(bf16)

    @pl.when(j == 1)
    def _():
        o_ref[...] = (z * QSCALE).astype(bf16)


def _proj_kvd_kernel(x_ref, w_ref, g_ref, gmat_ref, o32_ref, o16_ref):
    j = pl.program_id(0)
    z = jnp.dot(x_ref[...], w_ref[...], preferred_element_type=f32)

    @pl.when(j == 0)
    def _():
        zn = _qk_norm(z, g_ref[...], gmat_ref[...])
        o32_ref[...] = zn
        o16_ref[...] = zn.astype(bf16)

    @pl.when(j > 0)
    def _():
        o32_ref[...] = z
        o16_ref[...] = z.astype(bf16)


def _proj_kvs_kernel(x_ref, w_ref, ot_ref, o16_ref):
    z = jnp.dot(x_ref[...], w_ref[...], preferred_element_type=f32)
    ot_ref[...] = z.T
    o16_ref[...] = z.astype(bf16)


def _proj_gate_kernel(x_ref, w_ref, b_ref, o_ref):
    z = jnp.dot(x_ref[...], w_ref[...], preferred_element_type=f32) + b_ref[...]
    o_ref[...] = (1.0 / (1.0 + jnp.exp(-z))).astype(bf16)


def _proj_calls(xn, w_in, gq, gk, gmat, b_gate):
    n_rt = ROWS // TM_PROJ
    x_spec = pl.BlockSpec((TM_PROJ, D_MODEL), lambda j, i: (i, 0))
    vec_spec = pl.BlockSpec((1, COL), lambda j, i: (0, 0))
    gmat_spec = pl.BlockSpec((256, 256), lambda j, i: (0, 0))
    out_spec = pl.BlockSpec((TM_PROJ, COL), lambda j, i: (i, j))
    sem = ("arbitrary", "arbitrary")

    q = pl.pallas_call(
        _proj_q_kernel,
        grid=(2, n_rt),
        in_specs=[x_spec, pl.BlockSpec((D_MODEL, COL), lambda j, i: (0, 3 * j)), vec_spec, gmat_spec],
        out_specs=out_spec,
        out_shape=jax.ShapeDtypeStruct((ROWS, 2 * COL), bf16),
        compiler_params=_params(sem),
        name="proj_q",
    )(xn, w_in, gq, gmat)

    kvd32, kvd16 = pl.pallas_call(
        _proj_kvd_kernel,
        grid=(2, n_rt),
        in_specs=[x_spec, pl.BlockSpec((D_MODEL, COL), lambda j, i: (0, j + 1)), vec_spec, gmat_spec],
        out_specs=[out_spec, out_spec],
        out_shape=[jax.ShapeDtypeStruct((ROWS, 2 * COL), f32), jax.ShapeDtypeStruct((ROWS, 2 * COL), bf16)],
        compiler_params=_params(sem),
        name="proj_kvd",
    )(xn, w_in, gk, gmat)

    tps = SEQ // TM_KVS
    kst_p, ks16_p, kst_s, ks16_s = [], [], [], []
    for c in range(2):
        w_spec = pl.BlockSpec((D_MODEL, COL), lambda i, c=c: (0, 4 + c))
        t, r = pl.pallas_call(
            _proj_kvs_kernel,
            grid=(ROWS_P // TM_KVS,),
            in_specs=[pl.BlockSpec((TM_KVS, D_MODEL), lambda i: (i, 0)), w_spec],
            out_specs=[pl.BlockSpec((None, COL, TM_KVS), lambda i: (i // tps, 0, i % tps)),
                       pl.BlockSpec((TM_KVS, COL), lambda i: (i, 0))],
            out_shape=[jax.ShapeDtypeStruct((BATCH, COL, SEQ), f32), jax.ShapeDtypeStruct((ROWS_P, COL), bf16)],
            compiler_params=_params(("arbitrary",)),
            name="proj_kvs",
        )(xn, w_in)
        kst_p.append(t)
        ks16_p.append(r)
        t, r = pl.pallas_call(
            _proj_kvs_kernel,
            grid=(1,),
            in_specs=[pl.BlockSpec((SMALL, D_MODEL), lambda i: (ROWS_P // SMALL, 0)), w_spec],
            out_specs=[pl.BlockSpec((COL, SMALL), lambda i: (0, 0)), pl.BlockSpec((SMALL, COL), lambda i: (0, 0))],
            out_shape=[jax.ShapeDtypeStruct((COL, SMALL), f32), jax.ShapeDtypeStruct((SMALL, COL), bf16)],
            compiler_params=_params(("arbitrary",)),
            name="proj_kvs_small",
        )(xn, w_in)
        kst_s.append(t)
        ks16_s.append(r)

    gates = pl.pallas_call(
        _proj_gate_kernel,
        grid=(4, n_rt),
        in_specs=[x_spec, pl.BlockSpec((D_MODEL, COL), lambda j, i: (0, j + 6)),
                  pl.BlockSpec((1, COL), lambda j, i: (0, j))],
        out_specs=out_spec,
        out_shape=jax.ShapeDtypeStruct((ROWS, 4 * COL), bf16),
        compiler_params=_params(sem),
        name="proj_gate",
    )(xn, w_in, b_gate)
    return q, kvd32, kvd16, kst_p, ks16_p, kst_s, ks16_s, gates


def _nt_dot(a, b):
    return lax.dot_general(a, b, (((1,), (1,)), ((), ())), preferred_element_type=f32)


def _diff_update(parts, m_ref, l_ref, acc_ref, key_axis):
    m_old = m_ref[...]
    m_new = m_old
    for s, _ in parts:
        m_new = jnp.maximum(m_new, jnp.max(s, axis=key_axis, keepdims=True))
    alpha = jnp.exp(m_old - m_new)
    l_new = alpha * l_ref[...]
    acc = alpha * acc_ref[...]
    for s, pv_fn in parts:
        p = jnp.exp(s - m_new)
        l_new = l_new + jnp.sum(p, axis=key_axis, keepdims=True)
        acc = acc + pv_fn(p.astype(bf16))
    l_ref[...] = l_new
    acc_ref[...] = acc
    m_ref[...] = m_new


SIGN_BIT = -2 ** 31
SB_CUTOFF = 105.0


def _sb_update(parts, later_fn, c_ref, acc_ref, key_axis):
    pre = []
    for z, mask, _ in parts:
        neg_abs = lax.bitcast_convert_type(lax.bitcast_convert_type(z, jnp.int32) | SIGN_BIT, f32)
        sp = jnp.maximum(z, 0.0) + jnp.log(1.0 + jnp.exp(neg_abs))
        base = z - sp
        if mask is not None:
            sp = jnp.where(mask, sp, 0.0)
        hi = sp.astype(bf16)
        lo = (sp - hi.astype(f32)).astype(bf16)
        pre.append((base - (later_fn(hi) + later_fn(lo)), jnp.sum(sp, axis=key_axis, keepdims=True)))
    c = c_ref[...]
    acc = acc_ref[...]
    for (base, tot), (_, mask, pv_fn) in zip(pre, parts):
        w = jnp.exp(base - c)
        if mask is not None:
            w = jnp.where(mask, w, 0.0)
        acc = acc + pv_fn(w.astype(bf16))
        c = c + tot
    c_ref[...] = c
    acc_ref[...] = acc


def _bias_chain(dist, rb_of):
    b = jnp.where(dist >= BUCKET_START[1], rb_of(1), rb_of(0))
    for k in range(2, N_BUCKETS):
        b = jnp.where(dist >= BUCKET_START[k], rb_of(k), b)
    return b


def _lam(lq1, lk1, lq2, lk2):
    return (jnp.exp(jnp.sum(lq1 * lk1, axis=1, keepdims=True))
            - jnp.exp(jnp.sum(lq2 * lk2, axis=1, keepdims=True)) + LAM_INIT)


def _sub_norm(od, sg):
    ms = jnp.mean(od * od, axis=-1, keepdims=True)
    return (od * lax.rsqrt(ms + NORM_EPS) * sg) * (1.0 - LAM_INIT)


def _nn(v):
    return lambda p: jnp.dot(p, v, preferred_element_type=f32)


TQ = 256
NQ = SEQ // TQ
META_TILE = 128
assert TQ + 1 >= FAR_DIST and N_META + TQ - (N_META - 1) >= FAR_DIST


def _left(vt):
    return lambda p: jnp.dot(vt, p, preferred_element_type=f32)


def _transpose_bf16(x):
    return x.astype(f32).T.astype(bf16)


def _attn_kernel(rb_ref, qd_ref, qs_ref, kd_ref, vd_ref, ks_ref, vs_ref,
                 kdm_ref, vdm_ref, ksm_ref, vsm_ref,
                 lq1_ref, lk1_ref, lq2_ref, lk2_ref, sg_ref, a_ref, am_ref,
                 od_ref, os_ref,
                 b0_ref, b1_ref, bm_ref, vdt_ref, vst_ref, vdmt_ref, vsmt_ref,
                 md_ref, ld_ref, accd_ref, cs_ref, accs_ref):
    h = pl.program_id(0)
    qi = pl.program_id(2)
    R = 2 * TQ

    def rb_of(k):
        return rb_ref[k, h]

    key = lax.broadcasted_iota(jnp.int32, (TQ, R), 0)
    qry = lax.broadcasted_iota(jnp.int32, (TQ, R), 1) % TQ

    far_bias = rb_of(N_BUCKETS - 1)

    @pl.when(jnp.logical_and(pl.program_id(1) == 0, qi == 0))
    def _():
        b0_ref[...] = _bias_chain(qry - key, rb_of) - far_bias
        b1_ref[...] = _bias_chain(TQ + qry - key, rb_of) - far_bias
        mkey = lax.broadcasted_iota(jnp.int32, (N_META, R), 0)
        mqry = lax.broadcasted_iota(jnp.int32, (N_META, R), 1) % TQ
        bm_ref[...] = _bias_chain(N_META + mqry - mkey, rb_of) - far_bias

    @pl.when(qi == 0)
    def _():
        for j in range(NQ):
            vdt_ref[j] = _transpose_bf16(vd_ref[j * TQ:(j + 1) * TQ, :])
            vst_ref[j] = _transpose_bf16(vs_ref[j * TQ:(j + 1) * TQ, :])
        vdmt_ref[...] = _transpose_bf16(vdm_ref[...])
        vsmt_ref[...] = _transpose_bf16(vsm_ref[...])

    md_ref[...] = jnp.full((1, R), NEG_INF, f32)
    ld_ref[...] = jnp.zeros((1, R), f32)
    accd_ref[...] = jnp.zeros((128, R), f32)
    cs_ref[...] = jnp.zeros((1, R), f32)
    accs_ref[...] = jnp.zeros((128, R), f32)

    lane = lax.broadcasted_iota(jnp.int32, (1, 128), 1)
    lo_half = (lane < HD_DIFF).astype(f32).astype(bf16)
    hi_half = (lane >= HD_DIFF).astype(f32).astype(bf16)
    qd = qd_ref[...]
    qs = qs_ref[...]
    qd2 = jnp.concatenate([qd * lo_half, qd * hi_half], axis=0)
    qs2 = jnp.concatenate([qs * lo_half, qs * hi_half], axis=0)
    a = a_ref[...]

    def later(x):
        return jnp.dot(a, x, preferred_element_type=f32)

    def tiles(specs, with_sb, extra_diff=()):
        dparts = list(extra_diff)
        sparts = []
        for j, bias, dmask, smask in specs:
            sl = pl.ds(pl.multiple_of(j * TQ, TQ), TQ)
            s = _nt_dot(kd_ref[sl, :], qd2)
            if bias is not None:
                s = s + bias
            if dmask is not None:
                s = jnp.where(dmask, s, NEG_INF)
            dparts.append((s, _left(vdt_ref[j])))
            if with_sb:
                sparts.append((_nt_dot(ks_ref[sl, :], qs2), smask, _left(vst_ref[j])))
        _diff_update(dparts, md_ref, ld_ref, accd_ref, 0)
        if with_sb:
            _sb_update(sparts, later, cs_ref, accs_ref, 0)

    def far(js, with_sb):
        tiles([(j, None, None, None) for j in js], with_sb)

    meta_bias = bm_ref[...] * jnp.where(qi == 0, 1.0, 0.0)
    meta_diff = (_nt_dot(kdm_ref[0:N_META, :], qd2) + meta_bias, _left(vdmt_ref[:, 0:N_META]))
    diag = (qi, b0_ref[...], qry >= key, qry > key)

    @pl.when(qi == 0)
    def _():
        tiles([diag], True, [meta_diff])

    @pl.when(qi >= 1)
    def _():
        tiles([diag, (qi - 1, b1_ref[...], None, None)], True, [meta_diff])

    sb_live = jnp.min(cs_ref[...], axis=1, keepdims=True)[0, 0] < SB_CUTOFF
    n_far = jnp.maximum(qi - 1, 0)

    @pl.when(sb_live)
    def _():
        def pair(t, carry):
            j = qi - 2 - 2 * t
            far([j, j - 1], True)
            return carry

        lax.fori_loop(0, n_far // 2, pair, 0)

        @pl.when(n_far % 2 == 1)
        def _():
            far([0], True)

        am = am_ref[...]
        _sb_update([(_nt_dot(ksm_ref[0:N_META, :], qs2), None, _left(vsmt_ref[:, 0:N_META]))],
                   lambda x: jnp.dot(am, x, preferred_element_type=f32), cs_ref, accs_ref, 0)

    @pl.when(jnp.logical_not(sb_live))
    def _():
        def quad(t, carry):
            j = qi - 2 - 4 * t
            far([j, j - 1, j - 2, j - 3], False)
            return carry

        lax.fori_loop(0, n_far // 4, quad, 0)

        def single(t, carry):
            far([n_far % 4 - 1 - t], False)
            return carry

        lax.fori_loop(0, n_far % 4, single, 0)

    lam = _lam(lq1_ref[...], lk1_ref[...], lq2_ref[...], lk2_ref[...])
    o = accd_ref[...] / ld_ref[...]
    odt = o[:, :TQ] - lam * o[:, TQ:]
    od_ref[...] = _sub_norm(odt.T, sg_ref[...]).astype(bf16)
    accs = accs_ref[...]
    vrow = lax.broadcasted_iota(jnp.int32, (128, TQ), 0)
    os_ref[...] = jnp.where(vrow < HD_SB, accs[:, :TQ], accs[:, TQ:]).T.astype(bf16)


def _attn_call(rel_bias, q, kvd16, ks16_p, ks16_s, lq1, lk1, lq2, lk2, sg, a, am):
    R = 2 * TQ
    nh = H_DIFF
    assert META_ROW0 % META_TILE == 0 and (META_ROW0 - ROWS_P) % META_TILE == 0

    def qspec(off):
        return pl.BlockSpec((TQ, 128), lambda h, b, i: (b * NQ + i, off + h))

    def kvspec(off):
        return pl.BlockSpec((SEQ, 128), lambda h, b, i: (b, off + h))

    def mspec(off, row0):
        return pl.BlockSpec((META_TILE, 128), lambda h, b, i: (row0 // META_TILE, off + h))

    def const(shape):
        return pl.BlockSpec(shape, lambda h, b, i: (0, 0))

    out_spec = pl.BlockSpec((TQ, 128), lambda h, b, i: (b * NQ + i, h))
    return pl.pallas_call(
        _attn_kernel,
        grid=(nh, BATCH, NQ),
        in_specs=[pl.BlockSpec(memory_space=pltpu.SMEM),
                  qspec(0), qspec(nh),
                  kvspec(0), kvspec(nh), kvspec(0), kvspec(0),
                  mspec(0, META_ROW0), mspec(nh, META_ROW0),
                  mspec(0, META_ROW0 - ROWS_P), mspec(0, META_ROW0 - ROWS_P),
                  const((1, HD_DIFF)), const((1, HD_DIFF)), const((1, HD_DIFF)), const((1, HD_DIFF)),
                  const((1, 128)), const((TQ, TQ)), const((N_META, N_META))],
        out_specs=[out_spec, out_spec],
        out_shape=[jax.ShapeDtypeStruct((ROWS_P, W_DIFF), bf16), jax.ShapeDtypeStruct((ROWS_P, W_SB), bf16)],
        scratch_shapes=[pltpu.VMEM((TQ, R), f32), pltpu.VMEM((TQ, R), f32), pltpu.VMEM((N_META, R), f32),
                        pltpu.VMEM((NQ, 128, TQ), bf16), pltpu.VMEM((NQ, 128, TQ), bf16),
                        pltpu.VMEM((128, META_TILE), bf16), pltpu.VMEM((128, META_TILE), bf16),
                        pltpu.VMEM((1, R), f32), pltpu.VMEM((1, R), f32), pltpu.VMEM((128, R), f32),
                        pltpu.VMEM((1, R), f32), pltpu.VMEM((128, R), f32)],
        compiler_params=_params(("arbitrary", "arbitrary", "arbitrary")),
        name="attn_prompt",
    )(rel_bias, q, q, kvd16, kvd16, ks16_p[0], ks16_p[1], kvd16, kvd16, ks16_s[0], ks16_s[1],
      lq1, lk1, lq2, lk2, sg, a, am)


PPS = 8
PGROUP = 4
NSTEP = N_PAGES // PPS
GPS = PPS // PGROUP
assert N_PAGES % PPS == 0 and PPS % PGROUP == 0 and GPS % 2 == 0
RS = 128
PH = PAGE_SIZE * H_DIFF
assert H_DIFF * 2 * DEC_SEQ == RS and H_SB * DEC_SEQ == RS
assert PAGE_SIZE + 1 >= FAR_DIST


def _diag_blocks_sb(pv):
    return jnp.concatenate([pv[8 * h:8 * (h + 1), 128 * (h // 2):128 * (h // 2 + 1)] for h in range(H_SB)], axis=0)


def _sample_kernel(pt_ref, rbrow_ref, qx_ref, qs_ref, kdn_ref, vdn_ref, ksn_ref, vsn_ref, *rest):
    diff_refs = rest[:2 * PPS]
    (ksb_hbm, vsb_hbm, lq1_ref, lk1_ref, lq2_ref, lk2_ref, sg_ref, u_ref, od_ref, os_ref,
     alast_ref, afar_ref, anear_ref, anew_ref, md_ref, ld_ref, accd_ref, cs_ref, accs_ref,
     ksbuf, vsbuf, sem, live_ref) = rest[2 * PPS:]
    b = pl.program_id(0)
    s = pl.program_id(1)
    qx = qx_ref[...]
    qs = qs_ref[...]

    def sb_copies(bb, group, slot):
        cps = []
        for k in range(PGROUP):
            page = pt_ref[bb, N_PAGES - 1 - (group * PGROUP + k)]
            cps.append(pltpu.make_async_copy(ksb_hbm.at[page], ksbuf.at[slot, k], sem.at[slot]))
            cps.append(pltpu.make_async_copy(vsb_hbm.at[page], vsbuf.at[slot, k], sem.at[slot]))
        return cps

    @pl.when(jnp.logical_and(b == 0, s == 0))
    def _():
        for cp in sb_copies(0, 0, 0):
            cp.start()
        live_ref[0] = 1

    def rb_col(k):
        return rbrow_ref[:, k:k + 1]

    def diff_geometry(n_lanes, base):
        r = lax.broadcasted_iota(jnp.int32, (RS, n_lanes), 0)
        ln = lax.broadcasted_iota(jnp.int32, (RS, n_lanes), 1)
        same_head = (ln % H_DIFF) == (r // (2 * DEC_SEQ))
        dist = base + (r % DEC_SEQ) - (ln // H_DIFF)
        return same_head, dist

    def diff_part(kd, vd, add):
        return (_nt_dot(qx, kd.astype(bf16)) + add, _nn(vd.astype(bf16)))

    u = u_ref[...]

    def later(x):
        return jnp.dot(x, u, preferred_element_type=f32)

    far_bias = rb_col(N_BUCKETS - 1)

    @pl.when(jnp.logical_and(b == 0, s == 0))
    def _():
        same, dist = diff_geometry(DEC_SEQ * H_DIFF, 0)
        anew_ref[...] = jnp.where(same & (dist >= 0), _bias_chain(dist, rb_col) - far_bias, NEG_INF)
        same, dist = diff_geometry(PH, PAGE_SIZE)
        anear_ref[...] = jnp.where(same, _bias_chain(dist, rb_col) - far_bias, NEG_INF)
        afar_ref[...] = jnp.where(same, 0.0, NEG_INF)

    @pl.when(s == 0)
    def _():
        md_ref[...] = jnp.full((RS, 1), NEG_INF, f32)
        ld_ref[...] = jnp.zeros((RS, 1), f32)
        accd_ref[...] = jnp.zeros((RS, 128), f32)
        cs_ref[...] = jnp.zeros((RS, 1), f32)
        accs_ref[...] = jnp.zeros((RS, 128), f32)
        alast_ref[...] = anear_ref[...]
        _diff_update([diff_part(kdn_ref[...], vdn_ref[...], anew_ref[...])], md_ref, ld_ref, accd_ref, 1)
        zpad = jnp.zeros((PAGE_SIZE - DEC_SEQ, COL), f32)
        ksn = jnp.concatenate([ksn_ref[...].astype(f32), zpad], axis=0).astype(bf16)
        vsn = jnp.concatenate([vsn_ref[...].astype(f32), zpad], axis=0).astype(bf16)
        qidx = lax.broadcasted_iota(jnp.int32, (RS, PAGE_SIZE), 0) % DEC_SEQ
        key = lax.broadcasted_iota(jnp.int32, (RS, PAGE_SIZE), 1)
        smask = (qidx > key) & (key < DEC_SEQ)
        _sb_update([(_nt_dot(qs, ksn), smask,
                     lambda w: _diag_blocks_sb(jnp.dot(w, vsn, preferred_element_type=f32)))],
                   later, cs_ref, accs_ref, 1)

    @pl.when(s == 1)
    def _():
        alast_ref[...] = afar_ref[...]

    dparts = []
    for k in range(PPS):
        kd_ref, vd_ref = diff_refs[2 * k:2 * k + 2]
        dparts.append(diff_part(kd_ref[...], vd_ref[...], alast_ref[...] if k == 0 else afar_ref[...]))
    _diff_update(dparts, md_ref, ld_ref, accd_ref, 1)

    for g in range(GPS):
        group = s * GPS + g
        slot = g % 2
        live = live_ref[0] == 1

        @pl.when(live)
        def _():
            for cp in sb_copies(b, group, slot):
                cp.wait()
            sparts = []
            for k in range(PGROUP):
                vt = vsbuf[slot, k].astype(bf16)
                sparts.append((jnp.dot(qs, ksbuf[slot, k].astype(bf16), preferred_element_type=f32), None,
                               lambda w, vt=vt: _diag_blocks_sb(_nt_dot(w, vt))))
            _sb_update(sparts, later, cs_ref, accs_ref, 1)

        still_live = jnp.logical_and(live, jnp.min(cs_ref[...], axis=0, keepdims=True)[0, 0] < SB_CUTOFF)
        if g < GPS - 1:
            next_same_seq = still_live
            next_seq = False
        else:
            last = s == NSTEP - 1
            next_same_seq = jnp.logical_and(jnp.logical_not(last), still_live)
            next_seq = jnp.logical_and(last, b + 1 < DEC_BATCH)

            @pl.when(next_seq)
            def _():
                for cp in sb_copies(b + 1, 0, 0):
                    cp.start()

        @pl.when(next_same_seq)
        def _():
            for cp in sb_copies(b, group + 1, (g + 1) % 2):
                cp.start()

        live_ref[0] = jnp.logical_or(next_same_seq, next_seq).astype(jnp.int32)

    @pl.when(s == NSTEP - 1)
    def _():
        lam = _lam(lq1_ref[...], lk1_ref[...], lq2_ref[...], lk2_ref[...])
        o = accd_ref[...] / ld_ref[...]
        sg = sg_ref[...]
        heads = []
        for h in range(H_DIFF):
            od = o[16 * h:16 * h + 8] - lam * o[16 * h + 8:16 * h + 16]
            heads.append(_sub_norm(od, sg))
        od_ref[...] = jnp.concatenate(heads, axis=1).astype(bf16)
        accs = accs_ref[...]
        lane = lax.broadcasted_iota(jnp.int32, (1, 128), 1)
        pairs = [jnp.where(lane < HD_SB, accs[16 * t:16 * t + 8], accs[16 * t + 8:16 * t + 16])
                 for t in range(H_SB // 2)]
        os_ref[...] = jnp.concatenate(pairs, axis=1).astype(bf16)


def _sample_call(page_table, rbrow, qx, qbd_s, kdn, vdn, ksn, vsn, caches, lq1, lk1, lq2, lk2, sg, u):
    def const2(shape):
        return pl.BlockSpec(shape, lambda b, s, pt: (0, 0))

    def per_b(shape):
        return pl.BlockSpec((None,) + shape, lambda b, s, pt: (b, 0, 0))

    def cache_spec(k):
        return pl.BlockSpec((None, COL, PAGE_SIZE),
                            lambda b, s, pt: (pt[b, N_PAGES - 1 - (s * PPS + k)], 0, 0))

    in_specs = [const2((RS, N_BUCKETS)), per_b((RS, 128)), per_b((RS, COL)),
                per_b((DEC_SEQ * H_DIFF, 128)), per_b((DEC_SEQ * H_DIFF, 128)),
                per_b((DEC_SEQ, COL)), per_b((DEC_SEQ, COL))]
    args = [rbrow, qx, qbd_s, kdn, vdn, ksn, vsn]
    kd_cache, vd_cache, ks_cache, vs_cache = caches
    for k in range(PPS):
        for c in (kd_cache, vd_cache):
            in_specs.append(cache_spec(k))
            args.append(c)
    in_specs += [pl.BlockSpec(memory_space=pl.ANY)] * 2
    args += [ks_cache, vs_cache]
    in_specs += [const2((1, HD_DIFF))] * 4 + [const2((1, 128)), const2((PAGE_SIZE, PAGE_SIZE))]
    args += [lq1, lk1, lq2, lk2, sg, u]
    out_spec = per_b((DEC_SEQ, COL))
    grid_spec = pltpu.PrefetchScalarGridSpec(
        num_scalar_prefetch=1,
        grid=(DEC_BATCH, NSTEP),
        in_specs=in_specs,
        out_specs=[out_spec, out_spec],
        scratch_shapes=[pltpu.VMEM((RS, PH), f32), pltpu.VMEM((RS, PH), f32),
                        pltpu.VMEM((RS, PH), f32), pltpu.VMEM((RS, DEC_SEQ * H_DIFF), f32),
                        pltpu.VMEM((RS, 1), f32), pltpu.VMEM((RS, 1), f32), pltpu.VMEM((RS, 128), f32),
                        pltpu.VMEM((RS, 1), f32), pltpu.VMEM((RS, 128), f32),
                        pltpu.VMEM((2, PGROUP, COL, PAGE_SIZE), f32), pltpu.VMEM((2, PGROUP, COL, PAGE_SIZE), f32),
                        pltpu.SemaphoreType.DMA((2,)), pltpu.SMEM((1,), jnp.int32)],
    )
    return pl.pallas_call(
        _sample_kernel,
        grid_spec=grid_spec,
        out_shape=[jax.ShapeDtypeStruct((DEC_BATCH, DEC_SEQ, W_DIFF), bf16),
                   jax.ShapeDtypeStruct((DEC_BATCH, DEC_SEQ, W_SB), bf16)],
        compiler_params=_params(("arbitrary", "arbitrary")),
        name="attn_sample",
    )(page_table, *args)


TM_MIX = 256
NP_MIX = ROWS_P // TM_MIX


def _mix_kernel(odp_ref, osp_ref, ods_ref, oss_ref, g_ref, xp_ref, xs_ref,
                wa_ref, wb_ref, wo_ref, n2_ref, h2_ref, xn2_ref):
    i = pl.program_id(0)

    def body(od, osb, x):
        ya = jnp.dot(od, wa_ref[...], preferred_element_type=f32)
        yb = jnp.dot(osb, wb_ref[...], preferred_element_type=f32)
        g = g_ref[...]
        mix = g[:, :D_MODEL].astype(f32) * ya + g[:, D_MODEL:].astype(f32) * yb
        h2 = x + jnp.dot(mix.astype(bf16), wo_ref[...], preferred_element_type=f32)
        h2_ref[...] = h2
        xn2_ref[...] = _rms(h2, n2_ref[...]).astype(bf16)

    @pl.when(i < NP_MIX)
    def _():
        body(odp_ref[...], osp_ref[...], xp_ref[...])

    @pl.when(i >= NP_MIX)
    def _():
        body(ods_ref[...], oss_ref[...], xs_ref[...])


def _mix_call(od_p, os_p, od_s, os_s, gates, xp, xs, wa, wb, wo, n2):
    def prow(i):
        return (jnp.minimum(i, NP_MIX - 1), 0)

    def srow(i):
        return (jnp.maximum(i - NP_MIX, 0), 0)

    def full(shape):
        return pl.BlockSpec(shape, lambda i: (0, 0), pipeline_mode=pl.Buffered(1))

    return pl.pallas_call(
        _mix_kernel,
        grid=(ROWS // TM_MIX,),
        in_specs=[pl.BlockSpec((TM_MIX, W_DIFF), prow), pl.BlockSpec((TM_MIX, W_SB), prow),
                  pl.BlockSpec((TM_MIX, W_DIFF), srow), pl.BlockSpec((TM_MIX, W_SB), srow),
                  pl.BlockSpec((TM_MIX, 2 * D_MODEL), lambda i: (i, 0)),
                  pl.BlockSpec((TM_MIX, D_MODEL), prow), pl.BlockSpec((TM_MIX, D_MODEL), srow),
                  full((W_DIFF, D_MODEL)), full((W_SB, D_MODEL)), full((D_MODEL, D_MODEL)),
                  full((1, D_MODEL))],
        out_specs=[pl.BlockSpec((TM_MIX, D_MODEL), lambda i: (i, 0)),
                   pl.BlockSpec((TM_MIX, D_MODEL), lambda i: (i, 0))],
        out_shape=[jax.ShapeDtypeStruct((ROWS, D_MODEL), f32), jax.ShapeDtypeStruct((ROWS, D_MODEL), bf16)],
        compiler_params=_params(("arbitrary",)),
        name="mix_out",
    )(od_p, os_p, od_s, os_s, gates, xp, xs, wa, wb, wo, n2)


TM_MLP = 512
TF_MLP = 1024
assert ROWS_P % TM_MLP == 0 and ROWS_P % SMALL == 0 and D_FF % TF_MLP == 0


def _mlp_kernel(xn_ref, h2_ref, wu_ref, wd_ref, o_ref):
    f = pl.program_id(1)
    u = jnp.maximum(jnp.dot(xn_ref[...], wu_ref[...], preferred_element_type=f32), 0.0)
    part = jnp.dot((u * u).astype(bf16), wd_ref[...], preferred_element_type=f32)

    @pl.when(f == 0)
    def _():
        o_ref[...] = h2_ref[...] + part

    @pl.when(f > 0)
    def _():
        o_ref[...] += part


def _mlp_call(xn2, h2, wu, wd, tm, first_tile, n_tiles, name):
    return pl.pallas_call(
        _mlp_kernel,
        grid=(n_tiles, D_FF // TF_MLP),
        in_specs=[pl.BlockSpec((tm, D_MODEL), lambda i, f: (first_tile + i, 0)),
                  pl.BlockSpec((tm, D_MODEL), lambda i, f: (first_tile + i, 0)),
                  pl.BlockSpec((D_MODEL, TF_MLP), lambda i, f: (0, f)),
                  pl.BlockSpec((TF_MLP, D_MODEL), lambda i, f: (f, 0))],
        out_specs=pl.BlockSpec((tm, D_MODEL), lambda i, f: (i, 0)),
        out_shape=jax.ShapeDtypeStruct((n_tiles * tm, D_MODEL), f32),
        compiler_params=_params(("arbitrary", "arbitrary")),
        name=name,
    )(xn2, h2, wu, wd)


def _later_matrix(n, keys_on_rows=False):
    r = np.arange(n)
    m = r[:, None] > r[None, :]
    return jnp.asarray((m.T if keys_on_rows else m).astype(np.float32), dtype=bf16)


def _block_diag_queries(qrows, n_groups):
    gw = COL // n_groups
    grp = jnp.arange(COL, dtype=jnp.int32) // gw
    keep = grp[None, :] == jnp.arange(n_groups, dtype=jnp.int32)[:, None]
    out = jnp.where(keep[None, :, None, :], qrows[:, None, :, :], jnp.zeros((), qrows.dtype))
    return out.reshape(DEC_BATCH, n_groups * DEC_SEQ, COL)


def kernel(x_prompt, x_sample, cache_k_diff, cache_v_diff, cache_k_sb, cache_v_sb, page_table, meta_tokens,
           rel_bias, norm1_g, w_in, b_gate, qk_norm_q, qk_norm_k, lam_q1, lam_k1, lam_q2, lam_k2, subln_g,
           w_branch_a, w_branch_b, w_o, norm2_g, w_up, w_down):
    l = 0
    xp = x_prompt.reshape(ROWS_P, D_MODEL)
    xs = jnp.concatenate([x_sample.reshape(ROWS_S, D_MODEL), meta_tokens.astype(f32),
                          jnp.zeros((SMALL - ROWS_S - N_META, D_MODEL), f32)], axis=0)
    w_in16 = w_in[l].astype(bf16)
    wa16 = w_branch_a[l].astype(bf16)
    wb16 = w_branch_b[l].astype(bf16)
    wo16 = w_o[l].astype(bf16)
    wu16 = w_up[l].astype(bf16)
    wd16 = w_down[l].astype(bf16)
    gq = jnp.tile(qk_norm_q[l].astype(f32), COL // HD_DIFF)[None]
    gk = jnp.tile(qk_norm_k[l].astype(f32), COL // HD_DIFF)[None]
    grp = np.arange(256) // HD_DIFF
    gmat = jnp.asarray((grp[:, None] == grp[None, :]).astype(np.float32), dtype=bf16)
    lq1, lk1, lq2, lk2 = (a[l].astype(f32)[None] for a in (lam_q1, lam_k1, lam_q2, lam_k2))
    sg = subln_g[l].astype(f32)[None]
    rb = rel_bias.astype(f32)

    xn = _norm_call(xp, xs, norm1_g[l].astype(f32)[None])
    q, kvd32, kvd16, kst_p, ks16_p, kst_s, ks16_s, gates = _proj_calls(xn, w_in16, gq, gk, gmat,
                                                                        b_gate[l].astype(f32)[None])

    od_p, os_p = _attn_call(rb, q, kvd16, ks16_p, ks16_s, lq1, lk1, lq2, lk2, sg,
                            _later_matrix(TQ, keys_on_rows=True), _later_matrix(N_META, keys_on_rows=True))

    q_s = q[ROWS_P:ROWS_P + ROWS_S].reshape(DEC_BATCH, DEC_SEQ, 2 * COL)
    qd_s = jnp.transpose(q_s[..., :COL].reshape(DEC_BATCH, DEC_SEQ, H_DIFF, 2 * HD_DIFF), (0, 2, 1, 3))
    half = (jnp.arange(2 * HD_DIFF, dtype=jnp.int32) // HD_DIFF)[None, :] == jnp.arange(2, dtype=jnp.int32)[:, None]
    qx = jnp.where(half[None, None, :, None, :], qd_s[:, :, None], jnp.zeros((), bf16)).reshape(DEC_BATCH, RS, 128)
    qbd_s = _block_diag_queries(q_s[..., COL:], H_SB)
    new32 = kvd32[ROWS_P:ROWS_P + ROWS_S].reshape(DEC_BATCH, DEC_SEQ, 2 * COL)
    kdn = new32[..., 0:COL].reshape(DEC_BATCH, DEC_SEQ * H_DIFF, 2 * HD_DIFF)
    vdn = new32[..., COL:2 * COL].reshape(DEC_BATCH, DEC_SEQ * H_DIFF, 2 * HD_DIFF)
    ksn = ks16_s[0][:ROWS_S].reshape(DEC_BATCH, DEC_SEQ, COL)
    vsn = ks16_s[1][:ROWS_S].reshape(DEC_BATCH, DEC_SEQ, COL)
    n_phys = cache_k_diff.shape[1]
    caches = [cache_k_diff[l].reshape(n_phys, COL, 2 * HD_DIFF), cache_v_diff[l].reshape(n_phys, COL, 2 * HD_DIFF),
              jnp.transpose(cache_k_sb[l], (0, 2, 3, 1)).reshape(n_phys, COL, PAGE_SIZE),
              jnp.transpose(cache_v_sb[l], (0, 2, 3, 1)).reshape(n_phys, COL, PAGE_SIZE)]
    rbrow = jnp.repeat(rb.T, 2 * DEC_SEQ, axis=0)
    od_s, os_s = _sample_call(page_table, rbrow, qx, qbd_s, kdn, vdn, ksn, vsn, caches, lq1, lk1, lq2, lk2, sg,
                              _later_matrix(PAGE_SIZE))
    pad = jnp.zeros((SMALL - ROWS_S, COL), bf16)
    od_s = jnp.concatenate([od_s.reshape(ROWS_S, COL), pad], axis=0)
    os_s = jnp.concatenate([os_s.reshape(ROWS_S, COL), pad], axis=0)

    h2, xn2 = _mix_call(od_p, os_p, od_s, os_s, gates, xp, xs, wa16, wb16, wo16, norm2_g[l].astype(f32)[None])
    y_p = _mlp_call(xn2, h2, wu16, wd16, TM_MLP, 0, ROWS_P // TM_MLP, "mlp")
    y_s = _mlp_call(xn2, h2, wu16, wd16, SMALL, ROWS_P // SMALL, 1, "mlp_small")

    y_prompt = y_p.reshape(BATCH, SEQ, D_MODEL)
    y_sample = y_s[:ROWS_S].reshape(DEC_BATCH, DEC_SEQ, D_MODEL)

    def prompt_kvd(c):
        body = kvd32[:ROWS_P, c * COL:(c + 1) * COL].reshape(BATCH, SEQ, H_DIFF, 2 * HD_DIFF)
        meta = kvd32[META_ROW0:META_ROW0 + N_META, c * COL:(c + 1) * COL].reshape(1, N_META, H_DIFF, 2 * HD_DIFF)
        return jnp.concatenate([jnp.broadcast_to(meta, (BATCH, N_META, H_DIFF, 2 * HD_DIFF)), body], axis=1)[None]

    def sample_kvd(c):
        return kvd32[ROWS_P:ROWS_P + ROWS_S, c * COL:(c + 1) * COL].reshape(1, DEC_BATCH, DEC_SEQ, H_DIFF, 2 * HD_DIFF)

    m0 = META_ROW0 - ROWS_P

    def prompt_kvs(c):
        meta = jnp.broadcast_to(kst_s[c][:, m0:m0 + N_META][None], (BATCH, COL, N_META))
        full = jnp.concatenate([meta, kst_p[c]], axis=2).reshape(BATCH, H_SB, HD_SB, N_META + SEQ)
        return jnp.transpose(full, (0, 3, 1, 2))[None]

    def sample_kvs(c):
        return jnp.transpose(kst_s[c][:, :ROWS_S]).reshape(1, DEC_BATCH, DEC_SEQ, H_SB, HD_SB)

    return (y_prompt, y_sample,
            prompt_kvd(0), prompt_kvd(1), prompt_kvs(0), prompt_kvs(1),
            sample_kvd(0), sample_kvd(1), sample_kvs(0), sample_kvs(1))
```

```python
import functools
import math

import numpy as np
import jax
import jax.numpy as jnp
from jax import lax
from jax.experimental import pallas as pl
from jax.experimental.pallas import tpu as pltpu

f32 = jnp.float32
bf16 = jnp.bfloat16

D_MODEL = 2048
BATCH = 4
SEQ = 2048
DEC_BATCH = 32
DEC_SEQ = 8
PAST_LEN = 8192
PAGE_SIZE = 128
N_PAGES = PAST_LEN // PAGE_SIZE
N_META = 16
H_DIFF = 8
HD_DIFF = 64
W_DIFF = H_DIFF * 2 * HD_DIFF
H_SB = 16
HD_SB = 64
W_SB = H_SB * HD_SB
D_FF = 4 * D_MODEL
N_BUCKETS = 32
MAX_DISTANCE = 128
NORM_EPS = 1e-6
NEG_INF = -1e30
LAM_INIT = 0.8 - 0.6 * math.exp(-0.3 * 0)

ROWS_P = BATCH * SEQ
ROWS_S = DEC_BATCH * DEC_SEQ
SMALL = 512
ROWS = ROWS_P + SMALL
META_ROW0 = ROWS_P + ROWS_S
COL = 1024
QSCALE = HD_DIFF ** -0.5

VMEM_LIMIT = 56 * 1024 * 1024


def _bucket_thresholds():
    n = np.arange(0, 4 * MAX_DISTANCE)
    max_exact = N_BUCKETS // 2
    nf = np.maximum(n, 1).astype(np.float64)
    large = max_exact + (np.log(nf / max_exact) / math.log(MAX_DISTANCE / max_exact)
                         * (N_BUCKETS - max_exact)).astype(np.int64)
    bucket = np.where(n < max_exact, n, np.minimum(large, N_BUCKETS - 1))
    return [int(np.argmax(bucket >= b)) for b in range(N_BUCKETS)]


BUCKET_START = _bucket_thresholds()
FAR_DIST = BUCKET_START[-1]


def _params(sem):
    return pltpu.CompilerParams(dimension_semantics=sem, vmem_limit_bytes=VMEM_LIMIT)


TM_NORM = 512
NP_TILES = ROWS_P // TM_NORM


def _rms(x, g):
    ms = jnp.mean(x * x, axis=-1, keepdims=True)
    return x * lax.rsqrt(ms + NORM_EPS) * g


def _norm_kernel(xp_ref, xs_ref, g_ref, o_ref):
    i = pl.program_id(0)

    @pl.when(i < NP_TILES)
    def _():
        o_ref[...] = _rms(xp_ref[...], g_ref[...]).astype(bf16)

    @pl.when(i >= NP_TILES)
    def _():
        o_ref[...] = _rms(xs_ref[...], g_ref[...]).astype(bf16)


def _norm_call(xp, xs, g):
    return pl.pallas_call(
        _norm_kernel,
        grid=(ROWS // TM_NORM,),
        in_specs=[
            pl.BlockSpec((TM_NORM, D_MODEL), lambda i: (jnp.minimum(i, NP_TILES - 1), 0)),
            pl.BlockSpec((TM_NORM, D_MODEL), lambda i: (jnp.maximum(i - NP_TILES, 0), 0)),
            pl.BlockSpec((1, D_MODEL), lambda i: (0, 0)),
        ],
        out_specs=pl.BlockSpec((TM_NORM, D_MODEL), lambda i: (i, 0)),
        out_shape=jax.ShapeDtypeStruct((ROWS, D_MODEL), bf16),
        compiler_params=_params(("arbitrary",)),
        name="norm1",
    )(xp, xs, g)


TM_PROJ = 1088
TM_KVS = 1024
assert ROWS % TM_PROJ == 0 and SEQ % TM_KVS == 0 and ROWS_P % SMALL == 0


def _group_sumsq(z, gmat):
    zz = z * z
    hi = zz.astype(bf16)
    lo = (zz - hi.astype(f32)).astype(bf16)
    parts = []
    for c in range(z.shape[1] // 256):
        sl = slice(256 * c, 256 * (c + 1))
        parts.append(jnp.dot(hi[:, sl], gmat, preferred_element_type=f32)
                     + jnp.dot(lo[:, sl], gmat, preferred_element_type=f32))
    return jnp.concatenate(parts, axis=1)


def _qk_norm(z, g, gmat):
    ss = _group_sumsq(z, gmat)
    return z * lax.rsqrt(ss * (1.0 / HD_DIFF) + NORM_EPS) * g


def _proj_q_kernel(x_ref, w_ref, g_ref, gmat_ref, o_ref):
    j = pl.program_id(0)
    z = jnp.dot(x_ref[...], w_ref[...], preferred_element_type=f32)

    @pl.when(j == 0)
    def _():
        o_ref[...] = (_qk_norm(z, g_ref[...], gmat_ref[...]) * QSCALE).astype(bf16)

    @pl.when(j == 1)
    def _():
        o_ref[...] = (z * QSCALE).astype(bf16)


def _proj_kvd_kernel(x_ref, w_ref, g_ref, gmat_ref, o32_ref, o16_ref):
    j = pl.program_id(0)
    z = jnp.dot(x_ref[...], w_ref[...], preferred_element_type=f32)

    @pl.when(j == 0)
    def _():
        zn = _qk_norm(z, g_ref[...], gmat_ref[...])
        o32_ref[...] = zn
        o16_ref[...] = zn.astype(bf16)

    @pl.when(j > 0)
    def _():
        o32_ref[...] = z
        o16_ref[...] = z.astype(bf16)


def _proj_kvs_kernel(x_ref, w_ref, ot_ref, o16_ref):
    z = jnp.dot(x_ref[...], w_ref[...], preferred_element_type=f32)
    ot_ref[...] = z.T
    o16_ref[...] = z.astype(bf16)


def _proj_gate_kernel(x_ref, w_ref, b_ref, o_ref):
    z = jnp.dot(x_ref[...], w_ref[...], preferred_element_type=f32) + b_ref[...]
    o_ref[...] = (0.5 * jnp.tanh(0.5 * z) + 0.5).astype(bf16)


def _proj_calls(xn, w_in, gq, gk, gmat, b_gate):
    n_rt = ROWS // TM_PROJ
    x_spec = pl.BlockSpec((TM_PROJ, D_MODEL), lambda j, i: (i, 0))
    vec_spec = pl.BlockSpec((1, COL), lambda j, i: (0, 0))
    gmat_spec = pl.BlockSpec((256, 256), lambda j, i: (0, 0))
    out_spec = pl.BlockSpec((TM_PROJ, COL), lambda j, i: (i, j))
    sem = ("arbitrary", "arbitrary")

    q = pl.pallas_call(
        _proj_q_kernel,
        grid=(2, n_rt),
        in_specs=[x_spec, pl.BlockSpec((D_MODEL, COL), lambda j, i: (0, 3 * j)), vec_spec, gmat_spec],
        out_specs=out_spec,
        out_shape=jax.ShapeDtypeStruct((ROWS, 2 * COL), bf16),
        compiler_params=_params(sem),
        name="proj_q",
    )(xn, w_in, gq, gmat)

    kvd32, kvd16 = pl.pallas_call(
        _proj_kvd_kernel,
        grid=(2, n_rt),
        in_specs=[x_spec, pl.BlockSpec((D_MODEL, COL), lambda j, i: (0, j + 1)), vec_spec, gmat_spec],
        out_specs=[out_spec, out_spec],
        out_shape=[jax.ShapeDtypeStruct((ROWS, 2 * COL), f32), jax.ShapeDtypeStruct((ROWS, 2 * COL), bf16)],
        compiler_params=_params(sem),
        name="proj_kvd",
    )(xn, w_in, gk, gmat)

    tps = SEQ // TM_KVS
    kst_p, ks16_p, kst_s, ks16_s = [], [], [], []
    for c in range(2):
        w_spec = pl.BlockSpec((D_MODEL, COL), lambda i, c=c: (0, 4 + c))
        t, r = pl.pallas_call(
            _proj_kvs_kernel,
            grid=(ROWS_P // TM_KVS,),
            in_specs=[pl.BlockSpec((TM_KVS, D_MODEL), lambda i: (i, 0)), w_spec],
            out_specs=[pl.BlockSpec((None, COL, TM_KVS), lambda i: (i // tps, 0, i % tps)),
                       pl.BlockSpec((TM_KVS, COL), lambda i: (i, 0))],
            out_shape=[jax.ShapeDtypeStruct((BATCH, COL, SEQ), f32), jax.ShapeDtypeStruct((ROWS_P, COL), bf16)],
            compiler_params=_params(("arbitrary",)),
            name="proj_kvs",
        )(xn, w_in)
        kst_p.append(t)
        ks16_p.append(r)
        t, r = pl.pallas_call(
            _proj_kvs_kernel,
            grid=(1,),
            in_specs=[pl.BlockSpec((SMALL, D_MODEL), lambda i: (ROWS_P // SMALL, 0)), w_spec],
            out_specs=[pl.BlockSpec((COL, SMALL), lambda i: (0, 0)), pl.BlockSpec((SMALL, COL), lambda i: (0, 0))],
            out_shape=[jax.ShapeDtypeStruct((COL, SMALL), f32), jax.ShapeDtypeStruct((SMALL, COL), bf16)],
            compiler_params=_params(("arbitrary",)),
            name="proj_kvs_small",
        )(xn, w_in)
        kst_s.append(t)
        ks16_s.append(r)

    gates = pl.pallas_call(
        _proj_gate_kernel,
        grid=(4, n_rt),
        in_specs=[x_spec, pl.BlockSpec((D_MODEL, COL), lambda j, i: (0, j + 6)),
                  pl.BlockSpec((1, COL), lambda j, i: (0, j))],
        out_specs=out_spec,
        out_shape=jax.ShapeDtypeStruct((ROWS, 4 * COL), bf16),
        compiler_params=_params(sem),
        name="proj_gate",
    )(xn, w_in, b_gate)
    return q, kvd32, kvd16, kst_p, ks16_p, kst_s, ks16_s, gates


def _nt_dot(a, b):
    return lax.dot_general(a, b, (((1,), (1,)), ((), ())), preferred_element_type=f32)


def _diff_update(parts, m_ref, l_ref, acc_ref, key_axis):
    m_old = m_ref[...]
    m_new = m_old
    for s, _ in parts:
        m_new = jnp.maximum(m_new, jnp.max(s, axis=key_axis, keepdims=True))
    alpha = jnp.exp(m_old - m_new)
    l_new = alpha * l_ref[...]
    acc = alpha * acc_ref[...]
    for s, pv_fn in parts:
        p = jnp.exp(s - m_new)
        l_new = l_new + jnp.sum(p, axis=key_axis, keepdims=True)
        acc = acc + pv_fn(p.astype(bf16))
    l_ref[...] = l_new
    acc_ref[...] = acc
    m_ref[...] = m_new


SIGN_BIT = -2 ** 31
SB_CUTOFF = 105.0


def _sb_update(parts, later_fn, c_ref, acc_ref, key_axis):
    pre = []
    for z, mask, _ in parts:
        neg_abs = lax.bitcast_convert_type(lax.bitcast_convert_type(z, jnp.int32) | SIGN_BIT, f32)
        sp = jnp.maximum(z, 0.0) + jnp.log(1.0 + jnp.exp(neg_abs))
        base = z - sp
        if mask is not None:
            sp = jnp.where(mask, sp, 0.0)
        hi = sp.astype(bf16)
        lo = (sp - hi.astype(f32)).astype(bf16)
        pre.append((base - (later_fn(hi) + later_fn(lo)), jnp.sum(sp, axis=key_axis, keepdims=True)))
    c = c_ref[...]
    acc = acc_ref[...]
    for (base, tot), (_, mask, pv_fn) in zip(pre, parts):
        w = jnp.exp(base - c)
        if mask is not None:
            w = jnp.where(mask, w, 0.0)
        acc = acc + pv_fn(w.astype(bf16))
        c = c + tot
    c_ref[...] = c
    acc_ref[...] = acc


def _bias_chain(dist, rb_of):
    b = jnp.where(dist >= BUCKET_START[1], rb_of(1), rb_of(0))
    for k in range(2, N_BUCKETS):
        b = jnp.where(dist >= BUCKET_START[k], rb_of(k), b)
    return b


def _lam(lq1, lk1, lq2, lk2):
    return (jnp.exp(jnp.sum(lq1 * lk1, axis=1, keepdims=True))
            - jnp.exp(jnp.sum(lq2 * lk2, axis=1, keepdims=True)) + LAM_INIT)


def _sub_norm(od, sg):
    ms = jnp.mean(od * od, axis=-1, keepdims=True)
    return (od * lax.rsqrt(ms + NORM_EPS) * sg) * (1.0 - LAM_INIT)


def _nn(v):
    return lambda p: jnp.dot(p, v, preferred_element_type=f32)


TQ = 256
NQ = SEQ // TQ
META_TILE = 128
assert TQ + 1 >= FAR_DIST and N_META + TQ - (N_META - 1) >= FAR_DIST


def _left(vt):
    return lambda p: jnp.dot(vt, p, preferred_element_type=f32)


def _transpose_bf16(x):
    return x.astype(f32).T.astype(bf16)


def _attn_kernel(rb_ref, qd_ref, qs_ref, kd_ref, vd_ref, ks_ref, vs_ref,
                 kdm_ref, vdm_ref, ksm_ref, vsm_ref,
                 lq1_ref, lk1_ref, lq2_ref, lk2_ref, sg_ref, a_ref, am_ref,
                 od_ref, os_ref,
                 b0_ref, b1_ref, bm_ref, vdt_ref, vst_ref, vdmt_ref, vsmt_ref,
                 md_ref, ld_ref, accd_ref, cs_ref, accs_ref):
    h = pl.program_id(0)
    qi = pl.program_id(2)
    R = 2 * TQ

    def rb_of(k):
        return rb_ref[k, h]

    key = lax.broadcasted_iota(jnp.int32, (TQ, R), 0)
    qry = lax.broadcasted_iota(jnp.int32, (TQ, R), 1) % TQ

    far_bias = rb_of(N_BUCKETS - 1)

    @pl.when(jnp.logical_and(pl.program_id(1) == 0, qi == 0))
    def _():
        b0_ref[...] = _bias_chain(qry - key, rb_of) - far_bias
        b1_ref[...] = _bias_chain(TQ + qry - key, rb_of) - far_bias
        mkey = lax.broadcasted_iota(jnp.int32, (N_META, R), 0)
        mqry = lax.broadcasted_iota(jnp.int32, (N_META, R), 1) % TQ
        bm_ref[...] = _bias_chain(N_META + mqry - mkey, rb_of) - far_bias

    @pl.when(qi == 0)
    def _():
        for j in range(NQ):
            vdt_ref[j] = _transpose_bf16(vd_ref[j * TQ:(j + 1) * TQ, :])
            vst_ref[j] = _transpose_bf16(vs_ref[j * TQ:(j + 1) * TQ, :])
        vdmt_ref[...] = _transpose_bf16(vdm_ref[...])
        vsmt_ref[...] = _transpose_bf16(vsm_ref[...])

    md_ref[...] = jnp.full((1, R), NEG_INF, f32)
    ld_ref[...] = jnp.zeros((1, R), f32)
    accd_ref[...] = jnp.zeros((128, R), f32)
    cs_ref[...] = jnp.zeros((1, R), f32)
    accs_ref[...] = jnp.zeros((128, R), f32)

    lane = lax.broadcasted_iota(jnp.int32, (1, 128), 1)
    lo_half = (lane < HD_DIFF).astype(f32).astype(bf16)
    hi_half = (lane >= HD_DIFF).astype(f32).astype(bf16)
    qd = qd_ref[...]
    qs = qs_ref[...]
    qd2 = jnp.concatenate([qd * lo_half, qd * hi_half], axis=0)
    qs2 = jnp.concatenate([qs * lo_half, qs * hi_half], axis=0)
    a = a_ref[...]

    def later(x):
        return jnp.dot(a, x, preferred_element_type=f32)

    def tiles(specs, with_sb, extra_diff=()):
        dparts = list(extra_diff)
        sparts = []
        for j, bias, dmask, smask in specs:
            sl = pl.ds(pl.multiple_of(j * TQ, TQ), TQ)
            s = _nt_dot(kd_ref[sl, :], qd2)
            if bias is not None:
                s = s + bias
            if dmask is not None:
                s = jnp.where(dmask, s, NEG_INF)
            dparts.append((s, _left(vdt_ref[j])))
            if with_sb:
                sparts.append((_nt_dot(ks_ref[sl, :], qs2), smask, _left(vst_ref[j])))
        _diff_update(dparts, md_ref, ld_ref, accd_ref, 0)
        if with_sb:
            _sb_update(sparts, later, cs_ref, accs_ref, 0)

    def far(js, with_sb):
        tiles([(j, None, None, None) for j in js], with_sb)

    meta_bias = bm_ref[...] * jnp.where(qi == 0, 1.0, 0.0)
    meta_diff = (_nt_dot(kdm_ref[0:N_META, :], qd2) + meta_bias, _left(vdmt_ref[:, 0:N_META]))
    diag = (qi, b0_ref[...], qry >= key, qry > key)

    @pl.when(qi == 0)
    def _():
        tiles([diag], True, [meta_diff])

    @pl.when(qi >= 1)
    def _():
        tiles([diag, (qi - 1, b1_ref[...], None, None)], True, [meta_diff])

    sb_live = jnp.min(cs_ref[...], axis=1, keepdims=True)[0, 0] < SB_CUTOFF
    n_far = jnp.maximum(qi - 1, 0)

    @pl.when(sb_live)
    def _():
        def pair(t, carry):
            j = qi - 2 - 2 * t
            far([j, j - 1], True)
            return carry

        lax.fori_loop(0, n_far // 2, pair, 0)

        @pl.when(n_far % 2 == 1)
        def _():
            far([0], True)

        am = am_ref[...]
        _sb_update([(_nt_dot(ksm_ref[0:N_META, :], qs2), None, _left(vsmt_ref[:, 0:N_META]))],
                   lambda x: jnp.dot(am, x, preferred_element_type=f32), cs_ref, accs_ref, 0)

    @pl.when(jnp.logical_not(sb_live))
    def _():
        def quad(t, carry):
            j = qi - 2 - 4 * t
            far([j, j - 1, j - 2, j - 3], False)
            return carry

        lax.fori_loop(0, n_far // 4, quad, 0)
        rem = n_far % 4

        @pl.when(rem >= 2)
        def _():
            far([rem - 1, rem - 2], False)

        @pl.when(rem % 2 == 1)
        def _():
            far([0], False)

    lam = _lam(lq1_ref[...], lk1_ref[...], lq2_ref[...], lk2_ref[...])
    o = accd_ref[...] / ld_ref[...]
    odt = o[:, :TQ] - lam * o[:, TQ:]
    od_ref[...] = _sub_norm(odt.T, sg_ref[...]).astype(bf16)
    accs = accs_ref[...]
    vrow = lax.broadcasted_iota(jnp.int32, (128, TQ), 0)
    os_ref[...] = jnp.where(vrow < HD_SB, accs[:, :TQ], accs[:, TQ:]).T.astype(bf16)


def _attn_call(rel_bias, q, kvd16, ks16_p, ks16_s, lq1, lk1, lq2, lk2, sg, a, am):
    R = 2 * TQ
    nh = H_DIFF
    assert META_ROW0 % META_TILE == 0 and (META_ROW0 - ROWS_P) % META_TILE == 0

    def qspec(off):
        return pl.BlockSpec((TQ, 128), lambda h, b, i: (b * NQ + i, off + h))

    def kvspec(off):
        return pl.BlockSpec((SEQ, 128), lambda h, b, i: (b, off + h))

    def mspec(off, row0):
        return pl.BlockSpec((META_TILE, 128), lambda h, b, i: (row0 // META_TILE, off + h))

    def const(shape):
        return pl.BlockSpec(shape, lambda h, b, i: (0, 0))

    out_spec = pl.BlockSpec((TQ, 128), lambda h, b, i: (b * NQ + i, h))
    return pl.pallas_call(
        _attn_kernel,
        grid=(nh, BATCH, NQ),
        in_specs=[pl.BlockSpec(memory_space=pltpu.SMEM),
                  qspec(0), qspec(nh),
                  kvspec(0), kvspec(nh), kvspec(0), kvspec(0),
                  mspec(0, META_ROW0), mspec(nh, META_ROW0),
                  mspec(0, META_ROW0 - ROWS_P), mspec(0, META_ROW0 - ROWS_P),
                  const((1, HD_DIFF)), const((1, HD_DIFF)), const((1, HD_DIFF)), const((1, HD_DIFF)),
                  const((1, 128)), const((TQ, TQ)), const((N_META, N_META))],
        out_specs=[out_spec, out_spec],
        out_shape=[jax.ShapeDtypeStruct((ROWS_P, W_DIFF), bf16), jax.ShapeDtypeStruct((ROWS_P, W_SB), bf16)],
        scratch_shapes=[pltpu.VMEM((TQ, R), f32), pltpu.VMEM((TQ, R), f32), pltpu.VMEM((N_META, R), f32),
                        pltpu.VMEM((NQ, 128, TQ), bf16), pltpu.VMEM((NQ, 128, TQ), bf16),
                        pltpu.VMEM((128, META_TILE), bf16), pltpu.VMEM((128, META_TILE), bf16),
                        pltpu.VMEM((1, R), f32), pltpu.VMEM((1, R), f32), pltpu.VMEM((128, R), f32),
                        pltpu.VMEM((1, R), f32), pltpu.VMEM((128, R), f32)],
        compiler_params=_params(("arbitrary", "arbitrary", "arbitrary")),
        name="attn_prompt",
    )(rel_bias, q, q, kvd16, kvd16, ks16_p[0], ks16_p[1], kvd16, kvd16, ks16_s[0], ks16_s[1],
      lq1, lk1, lq2, lk2, sg, a, am)


PPS = 8
PGROUP = 4
NSTEP = N_PAGES // PPS
GPS = PPS // PGROUP
assert N_PAGES % PPS == 0 and PPS % PGROUP == 0 and GPS % 2 == 0
RS = 128
PH = PAGE_SIZE * H_DIFF
assert H_DIFF * 2 * DEC_SEQ == RS and H_SB * DEC_SEQ == RS
assert PAGE_SIZE + 1 >= FAR_DIST


def _diag_blocks_sb(pv):
    return jnp.concatenate([pv[8 * h:8 * (h + 1), 128 * (h // 2):128 * (h // 2 + 1)] for h in range(H_SB)], axis=0)


def _sample_kernel(pt_ref, rbrow_ref, qx_ref, qs_ref, kdn_ref, vdn_ref, ksn_ref, vsn_ref, *rest):
    diff_refs = rest[:2 * PPS]
    (ksb_hbm, vsb_hbm, lq1_ref, lk1_ref, lq2_ref, lk2_ref, sg_ref, u_ref, od_ref, os_ref,
     alast_ref, afar_ref, anear_ref, anew_ref, md_ref, ld_ref, accd_ref, cs_ref, accs_ref,
     ksbuf, vsbuf, sem, live_ref) = rest[2 * PPS:]
    b = pl.program_id(0)
    s = pl.program_id(1)
    qx = qx_ref[...]
    qs = qs_ref[...]

    def sb_copies(bb, group, slot):
        cps = []
        for k in range(PGROUP):
            page = pt_ref[bb, N_PAGES - 1 - (group * PGROUP + k)]
            cps.append(pltpu.make_async_copy(ksb_hbm.at[page], ksbuf.at[slot, k], sem.at[slot]))
            cps.append(pltpu.make_async_copy(vsb_hbm.at[page], vsbuf.at[slot, k], sem.at[slot]))
        return cps

    @pl.when(jnp.logical_and(b == 0, s == 0))
    def _():
        for cp in sb_copies(0, 0, 0):
            cp.start()
        live_ref[0] = 1

    def rb_col(k):
        return rbrow_ref[:, k:k + 1]

    def diff_geometry(n_lanes, base):
        r = lax.broadcasted_iota(jnp.int32, (RS, n_lanes), 0)
        ln = lax.broadcasted_iota(jnp.int32, (RS, n_lanes), 1)
        same_head = (ln % H_DIFF) == (r // (2 * DEC_SEQ))
        dist = base + (r % DEC_SEQ) - (ln // H_DIFF)
        return same_head, dist

    def diff_part(kd, vd, add):
        return (_nt_dot(qx, kd.astype(bf16)) + add, _nn(vd.astype(bf16)))

    u = u_ref[...]

    def later(x):
        return jnp.dot(x, u, preferred_element_type=f32)

    far_bias = rb_col(N_BUCKETS - 1)

    @pl.when(jnp.logical_and(b == 0, s == 0))
    def _():
        same, dist = diff_geometry(DEC_SEQ * H_DIFF, 0)
        anew_ref[...] = jnp.where(same & (dist >= 0), _bias_chain(dist, rb_col) - far_bias, NEG_INF)
        same, dist = diff_geometry(PH, PAGE_SIZE)
        anear_ref[...] = jnp.where(same, _bias_chain(dist, rb_col) - far_bias, NEG_INF)
        afar_ref[...] = jnp.where(same, 0.0, NEG_INF)

    @pl.when(s == 0)
    def _():
        md_ref[...] = jnp.full((RS, 1), NEG_INF, f32)
        ld_ref[...] = jnp.zeros((RS, 1), f32)
        accd_ref[...] = jnp.zeros((RS, 128), f32)
        cs_ref[...] = jnp.zeros((RS, 1), f32)
        accs_ref[...] = jnp.zeros((RS, 128), f32)
        alast_ref[...] = anear_ref[...]
        _diff_update([diff_part(kdn_ref[...], vdn_ref[...], anew_ref[...])], md_ref, ld_ref, accd_ref, 1)
        zpad = jnp.zeros((PAGE_SIZE - DEC_SEQ, COL), f32)
        ksn = jnp.concatenate([ksn_ref[...].astype(f32), zpad], axis=0).astype(bf16)
        vsn = jnp.concatenate([vsn_ref[...].astype(f32), zpad], axis=0).astype(bf16)
        qidx = lax.broadcasted_iota(jnp.int32, (RS, PAGE_SIZE), 0) % DEC_SEQ
        key = lax.broadcasted_iota(jnp.int32, (RS, PAGE_SIZE), 1)
        smask = (qidx > key) & (key < DEC_SEQ)
        _sb_update([(_nt_dot(qs, ksn), smask,
                     lambda w: _diag_blocks_sb(jnp.dot(w, vsn, preferred_element_type=f32)))],
                   later, cs_ref, accs_ref, 1)

    @pl.when(s == 1)
    def _():
        alast_ref[...] = afar_ref[...]

    dparts = []
    for k in range(PPS):
        kd_ref, vd_ref = diff_refs[2 * k:2 * k + 2]
        dparts.append(diff_part(kd_ref[...], vd_ref[...], alast_ref[...] if k == 0 else afar_ref[...]))
    _diff_update(dparts, md_ref, ld_ref, accd_ref, 1)

    for g in range(GPS):
        group = s * GPS + g
        slot = g % 2
        live = live_ref[0] == 1

        @pl.when(live)
        def _():
            for cp in sb_copies(b, group, slot):
                cp.wait()
            sparts = []
            for k in range(PGROUP):
                vt = vsbuf[slot, k].astype(bf16)
                sparts.append((jnp.dot(qs, ksbuf[slot, k].astype(bf16), preferred_element_type=f32), None,
                               lambda w, vt=vt: _diag_blocks_sb(_nt_dot(w, vt))))
            _sb_update(sparts, later, cs_ref, accs_ref, 1)

        still_live = jnp.logical_and(live, jnp.min(cs_ref[...], axis=0, keepdims=True)[0, 0] < SB_CUTOFF)
        if g < GPS - 1:
            next_same_seq = still_live
            next_seq = False
        else:
            last = s == NSTEP - 1
            next_same_seq = jnp.logical_and(jnp.logical_not(last), still_live)
            next_seq = jnp.logical_and(last, b + 1 < DEC_BATCH)

            @pl.when(next_seq)
            def _():
                for cp in sb_copies(b + 1, 0, 0):
                    cp.start()

        @pl.when(next_same_seq)
        def _():
            for cp in sb_copies(b, group + 1, (g + 1) % 2):
                cp.start()

        live_ref[0] = jnp.logical_or(next_same_seq, next_seq).astype(jnp.int32)

    @pl.when(s == NSTEP - 1)
    def _():
        lam = _lam(lq1_ref[...], lk1_ref[...], lq2_ref[...], lk2_ref[...])
        o = accd_ref[...] / ld_ref[...]
        sg = sg_ref[...]
        heads = []
        for h in range(H_DIFF):
            od = o[16 * h:16 * h + 8] - lam * o[16 * h + 8:16 * h + 16]
            heads.append(_sub_norm(od, sg))
        od_ref[...] = jnp.concatenate(heads, axis=1).astype(bf16)
        accs = accs_ref[...]
        lane = lax.broadcasted_iota(jnp.int32, (1, 128), 1)
        pairs = [jnp.where(lane < HD_SB, accs[16 * t:16 * t + 8], accs[16 * t + 8:16 * t + 16])
                 for t in range(H_SB // 2)]
        os_ref[...] = jnp.concatenate(pairs, axis=1).astype(bf16)


def _sample_call(page_table, rbrow, qx, qbd_s, kdn, vdn, ksn, vsn, caches, lq1, lk1, lq2, lk2, sg, u):
    def const2(shape):
        return pl.BlockSpec(shape, lambda b, s, pt: (0, 0))

    def per_b(shape):
        return pl.BlockSpec((None,) + shape, lambda b, s, pt: (b, 0, 0))

    def cache_spec(k):
        return pl.BlockSpec((None, COL, PAGE_SIZE),
                            lambda b, s, pt: (pt[b, N_PAGES - 1 - (s * PPS + k)], 0, 0))

    in_specs = [const2((RS, N_BUCKETS)), per_b((RS, 128)), per_b((RS, COL)),
                per_b((DEC_SEQ * H_DIFF, 128)), per_b((DEC_SEQ * H_DIFF, 128)),
                per_b((DEC_SEQ, COL)), per_b((DEC_SEQ, COL))]
    args = [rbrow, qx, qbd_s, kdn, vdn, ksn, vsn]
    kd_cache, vd_cache, ks_cache, vs_cache = caches
    for k in range(PPS):
        for c in (kd_cache, vd_cache):
            in_specs.append(cache_spec(k))
            args.append(c)
    in_specs += [pl.BlockSpec(memory_space=pl.ANY)] * 2
    args += [ks_cache, vs_cache]
    in_specs += [const2((1, HD_DIFF))] * 4 + [const2((1, 128)), const2((PAGE_SIZE, PAGE_SIZE))]
    args += [lq1, lk1, lq2, lk2, sg, u]
    out_spec = per_b((DEC_SEQ, COL))
    grid_spec = pltpu.PrefetchScalarGridSpec(
        num_scalar_prefetch=1,
        grid=(DEC_BATCH, NSTEP),
        in_specs=in_specs,
        out_specs=[out_spec, out_spec],
        scratch_shapes=[pltpu.VMEM((RS, PH), f32), pltpu.VMEM((RS, PH), f32),
                        pltpu.VMEM((RS, PH), f32), pltpu.VMEM((RS, DEC_SEQ * H_DIFF), f32),
                        pltpu.VMEM((RS, 1), f32), pltpu.VMEM((RS, 1), f32), pltpu.VMEM((RS, 128), f32),
                        pltpu.VMEM((RS, 1), f32), pltpu.VMEM((RS, 128), f32),
                        pltpu.VMEM((2, PGROUP, COL, PAGE_SIZE), f32), pltpu.VMEM((2, PGROUP, COL, PAGE_SIZE), f32),
                        pltpu.SemaphoreType.DMA((2,)), pltpu.SMEM((1,), jnp.int32)],
    )
    return pl.pallas_call(
        _sample_kernel,
        grid_spec=grid_spec,
        out_shape=[jax.ShapeDtypeStruct((DEC_BATCH, DEC_SEQ, W_DIFF), bf16),
                   jax.ShapeDtypeStruct((DEC_BATCH, DEC_SEQ, W_SB), bf16)],
        compiler_params=_params(("arbitrary", "arbitrary")),
        name="attn_sample",
    )(page_table, *args)


TM_MIX = 256
NP_MIX = ROWS_P // TM_MIX


def _mix_kernel(odp_ref, osp_ref, ods_ref, oss_ref, g_ref, xp_ref, xs_ref,
                wa_ref, wb_ref, wo_ref, n2_ref, h2_ref, xn2_ref):
    i = pl.program_id(0)

    def body(od, osb, x):
        ya = jnp.dot(od, wa_ref[...], preferred_element_type=f32)
        yb = jnp.dot(osb, wb_ref[...], preferred_element_type=f32)
        g = g_ref[...]
        mix = g[:, :D_MODEL].astype(f32) * ya + g[:, D_MODEL:].astype(f32) * yb
        h2 = x + jnp.dot(mix.astype(bf16), wo_ref[...], preferred_element_type=f32)
        h2_ref[...] = h2
        xn2_ref[...] = _rms(h2, n2_ref[...]).astype(bf16)

    @pl.when(i < NP_MIX)
    def _():
        body(odp_ref[...], osp_ref[...], xp_ref[...])

    @pl.when(i >= NP_MIX)
    def _():
        body(ods_ref[...], oss_ref[...], xs_ref[...])


def _mix_call(od_p, os_p, od_s, os_s, gates, xp, xs, wa, wb, wo, n2):
    def prow(i):
        return (jnp.minimum(i, NP_MIX - 1), 0)

    def srow(i):
        return (jnp.maximum(i - NP_MIX, 0), 0)

    def full(shape):
        return pl.BlockSpec(shape, lambda i: (0, 0), pipeline_mode=pl.Buffered(1))

    return pl.pallas_call(
        _mix_kernel,
        grid=(ROWS // TM_MIX,),
        in_specs=[pl.BlockSpec((TM_MIX, W_DIFF), prow), pl.BlockSpec((TM_MIX, W_SB), prow),
                  pl.BlockSpec((TM_MIX, W_DIFF), srow), pl.BlockSpec((TM_MIX, W_SB), srow),
                  pl.BlockSpec((TM_MIX, 2 * D_MODEL), lambda i: (i, 0)),
                  pl.BlockSpec((TM_MIX, D_MODEL), prow), pl.BlockSpec((TM_MIX, D_MODEL), srow),
                  full((W_DIFF, D_MODEL)), full((W_SB, D_MODEL)), full((D_MODEL, D_MODEL)),
                  full((1, D_MODEL))],
        out_specs=[pl.BlockSpec((TM_MIX, D_MODEL), lambda i: (i, 0)),
                   pl.BlockSpec((TM_MIX, D_MODEL), lambda i: (i, 0))],
        out_shape=[jax.ShapeDtypeStruct((ROWS, D_MODEL), f32), jax.ShapeDtypeStruct((ROWS, D_MODEL), bf16)],
        compiler_params=_params(("arbitrary",)),
        name="mix_out",
    )(od_p, os_p, od_s, os_s, gates, xp, xs, wa, wb, wo, n2)


TM_MLP = 512
TF_MLP = 1024
assert ROWS_P % TM_MLP == 0 and ROWS_P % SMALL == 0 and D_FF % TF_MLP == 0


def _mlp_kernel(xn_ref, h2_ref, wu_ref, wd_ref, o_ref):
    f = pl.program_id(1)
    u = jnp.maximum(jnp.dot(xn_ref[...], wu_ref[...], preferred_element_type=f32), 0.0)
    part = jnp.dot((u * u).astype(bf16), wd_ref[...], preferred_element_type=f32)

    @pl.when(f == 0)
    def _():
        o_ref[...] = h2_ref[...] + part

    @pl.when(f > 0)
    def _():
        o_ref[...] += part


def _mlp_call(xn2, h2, wu, wd, tm, first_tile, n_tiles, name):
    return pl.pallas_call(
        _mlp_kernel,
        grid=(n_tiles, D_FF // TF_MLP),
        in_specs=[pl.BlockSpec((tm, D_MODEL), lambda i, f: (first_tile + i, 0)),
                  pl.BlockSpec((tm, D_MODEL), lambda i, f: (first_tile + i, 0)),
                  pl.BlockSpec((D_MODEL, TF_MLP), lambda i, f: (0, f)),
                  pl.BlockSpec((TF_MLP, D_MODEL), lambda i, f: (f, 0))],
        out_specs=pl.BlockSpec((tm, D_MODEL), lambda i, f: (i, 0)),
        out_shape=jax.ShapeDtypeStruct((n_tiles * tm, D_MODEL), f32),
        compiler_params=_params(("arbitrary", "arbitrary")),
        name=name,
    )(xn2, h2, wu, wd)


def _later_matrix(n, keys_on_rows=False):
    r = np.arange(n)
    m = r[:, None] > r[None, :]
    return jnp.asarray((m.T if keys_on_rows else m).astype(np.float32), dtype=bf16)


def _block_diag_queries(qrows, n_groups):
    gw = COL // n_groups
    grp = jnp.arange(COL, dtype=jnp.int32) // gw
    keep = grp[None, :] == jnp.arange(n_groups, dtype=jnp.int32)[:, None]
    out = jnp.where(keep[None, :, None, :], qrows[:, None, :, :], jnp.zeros((), qrows.dtype))
    return out.reshape(DEC_BATCH, n_groups * DEC_SEQ, COL)


def kernel(x_prompt, x_sample, cache_k_diff, cache_v_diff, cache_k_sb, cache_v_sb, page_table, meta_tokens,
           rel_bias, norm1_g, w_in, b_gate, qk_norm_q, qk_norm_k, lam_q1, lam_k1, lam_q2, lam_k2, subln_g,
           w_branch_a, w_branch_b, w_o, norm2_g, w_up, w_down):
    l = 0
    xp = x_prompt.reshape(ROWS_P, D_MODEL)
    xs = jnp.concatenate([x_sample.reshape(ROWS_S, D_MODEL), meta_tokens.astype(f32),
                          jnp.zeros((SMALL - ROWS_S - N_META, D_MODEL), f32)], axis=0)
    w_in16 = w_in[l].astype(bf16)
    wa16 = w_branch_a[l].astype(bf16)
    wb16 = w_branch_b[l].astype(bf16)
    wo16 = w_o[l].astype(bf16)
    wu16 = w_up[l].astype(bf16)
    wd16 = w_down[l].astype(bf16)
    gq = jnp.tile(qk_norm_q[l].astype(f32), COL // HD_DIFF)[None]
    gk = jnp.tile(qk_norm_k[l].astype(f32), COL // HD_DIFF)[None]
    grp = np.arange(256) // HD_DIFF
    gmat = jnp.asarray((grp[:, None] == grp[None, :]).astype(np.float32), dtype=bf16)
    lq1, lk1, lq2, lk2 = (a[l].astype(f32)[None] for a in (lam_q1, lam_k1, lam_q2, lam_k2))
    sg = subln_g[l].astype(f32)[None]
    rb = rel_bias.astype(f32)

    xn = _norm_call(xp, xs, norm1_g[l].astype(f32)[None])
    q, kvd32, kvd16, kst_p, ks16_p, kst_s, ks16_s, gates = _proj_calls(xn, w_in16, gq, gk, gmat,
                                                                        b_gate[l].astype(f32)[None])

    od_p, os_p = _attn_call(rb, q, kvd16, ks16_p, ks16_s, lq1, lk1, lq2, lk2, sg,
                            _later_matrix(TQ, keys_on_rows=True), _later_matrix(N_META, keys_on_rows=True))

    q_s = q[ROWS_P:ROWS_P + ROWS_S].reshape(DEC_BATCH, DEC_SEQ, 2 * COL)
    qd_s = jnp.transpose(q_s[..., :COL].reshape(DEC_BATCH, DEC_SEQ, H_DIFF, 2 * HD_DIFF), (0, 2, 1, 3))
    half = (jnp.arange(2 * HD_DIFF, dtype=jnp.int32) // HD_DIFF)[None, :] == jnp.arange(2, dtype=jnp.int32)[:, None]
    qx = jnp.where(half[None, None, :, None, :], qd_s[:, :, None], jnp.zeros((), bf16)).reshape(DEC_BATCH, RS, 128)
    qbd_s = _block_diag_queries(q_s[..., COL:], H_SB)
    new32 = kvd32[ROWS_P:ROWS_P + ROWS_S].reshape(DEC_BATCH, DEC_SEQ, 2 * COL)
    kdn = new32[..., 0:COL].reshape(DEC_BATCH, DEC_SEQ * H_DIFF, 2 * HD_DIFF)
    vdn = new32[..., COL:2 * COL].reshape(DEC_BATCH, DEC_SEQ * H_DIFF, 2 * HD_DIFF)
    ksn = ks16_s[0][:ROWS_S].reshape(DEC_BATCH, DEC_SEQ, COL)
    vsn = ks16_s[1][:ROWS_S].reshape(DEC_BATCH, DEC_SEQ, COL)
    n_phys = cache_k_diff.shape[1]
    caches = [cache_k_diff[l].reshape(n_phys, COL, 2 * HD_DIFF), cache_v_diff[l].reshape(n_phys, COL, 2 * HD_DIFF),
              jnp.transpose(cache_k_sb[l], (0, 2, 3, 1)).reshape(n_phys, COL, PAGE_SIZE),
              jnp.transpose(cache_v_sb[l], (0, 2, 3, 1)).reshape(n_phys, COL, PAGE_SIZE)]
    rbrow = jnp.repeat(rb.T, 2 * DEC_SEQ, axis=0)
    od_s, os_s = _sample_call(page_table, rbrow, qx, qbd_s, kdn, vdn, ksn, vsn, caches, lq1, lk1, lq2, lk2, sg,
                              _later_matrix(PAGE_SIZE))
    pad = jnp.zeros((SMALL - ROWS_S, COL), bf16)
    od_s = jnp.concatenate([od_s.reshape(ROWS_S, COL), pad], axis=0)
    os_s = jnp.concatenate([os_s.reshape(ROWS_S, COL), pad], axis=0)

    h2, xn2 = _mix_call(od_p, os_p, od_s, os_s, gates, xp, xs, wa16, wb16, wo16, norm2_g[l].astype(f32)[None])
    y_p = _mlp_call(xn2, h2, wu16, wd16, TM_MLP, 0, ROWS_P // TM_MLP, "mlp")
    y_s = _mlp_call(xn2, h2, wu16, wd16, SMALL, ROWS_P // SMALL, 1, "mlp_small")

    y_prompt = y_p.reshape(BATCH, SEQ, D_MODEL)
    y_sample = y_s[:ROWS_S].reshape(DEC_BATCH, DEC_SEQ, D_MODEL)

    def prompt_kvd(c):
        body = kvd32[:ROWS_P, c * COL:(c + 1) * COL].reshape(BATCH, SEQ, H_DIFF, 2 * HD_DIFF)
        meta = kvd32[META_ROW0:META_ROW0 + N_META, c * COL:(c + 1) * COL].reshape(1, N_META, H_DIFF, 2 * HD_DIFF)
        return jnp.concatenate([jnp.broadcast_to(meta, (BATCH, N_META, H_DIFF, 2 * HD_DIFF)), body], axis=1)[None]

    def sample_kvd(c):
        return kvd32[ROWS_P:ROWS_P + ROWS_S, c * COL:(c + 1) * COL].reshape(1, DEC_BATCH, DEC_SEQ, H_DIFF, 2 * HD_DIFF)

    m0 = META_ROW0 - ROWS_P

    def prompt_kvs(c):
        meta = jnp.broadcast_to(kst_s[c][:, m0:m0 + N_META][None], (BATCH, COL, N_META))
        full = jnp.concatenate([meta, kst_p[c]], axis=2).reshape(BATCH, H_SB, HD_SB, N_META + SEQ)
        return jnp.transpose(full, (0, 3, 1, 2))[None]

    def sample_kvs(c):
        return jnp.transpose(kst_s[c][:, :ROWS_S]).reshape(1, DEC_BATCH, DEC_SEQ, H_SB, HD_SB)

    return (y_prompt, y_sample,
            prompt_kvd(0), prompt_kvd(1), prompt_kvs(0), prompt_kvs(1),
            sample_kvd(0), sample_kvd(1), sample_kvs(0), sample_kvs(1))
```

```python
import functools
import math

import numpy as np
import jax
import jax.numpy as jnp
from jax import lax
from jax.experimental import pallas as pl
from jax.experimental.pallas import tpu as pltpu

f32 = jnp.float32
bf16 = jnp.bfloat16

D_MODEL = 2048
BATCH = 4
SEQ = 2048
DEC_BATCH = 32
DEC_SEQ = 8
PAST_LEN = 8192
PAGE_SIZE = 128
N_PAGES = PAST_LEN // PAGE_SIZE
N_META = 16
H_DIFF = 8
HD_DIFF = 64
W_DIFF = H_DIFF * 2 * HD_DIFF
H_SB = 16
HD_SB = 64
W_SB = H_SB * HD_SB
D_FF = 4 * D_MODEL
N_BUCKETS = 32
MAX_DISTANCE = 128
NORM_EPS = 1e-6
NEG_INF = -1e30
LAM_INIT = 0.8 - 0.6 * math.exp(-0.3 * 0)

ROWS_P = BATCH * SEQ
ROWS_S = DEC_BATCH * DEC_SEQ
SMALL = 512
ROWS = ROWS_P + SMALL
META_ROW0 = ROWS_P + ROWS_S
COL = 1024
QSCALE = HD_DIFF ** -0.5

VMEM_LIMIT = 56 * 1024 * 1024


def _bucket_thresholds():
    n = np.arange(0, 4 * MAX_DISTANCE)
    max_exact = N_BUCKETS // 2
    nf = np.maximum(n, 1).astype(np.float64)
    large = max_exact + (np.log(nf / max_exact) / math.log(MAX_DISTANCE / max_exact)
                         * (N_BUCKETS - max_exact)).astype(np.int64)
    bucket = np.where(n < max_exact, n, np.minimum(large, N_BUCKETS - 1))
    return [int(np.argmax(bucket >= b)) for b in range(N_BUCKETS)]


BUCKET_START = _bucket_thresholds()
FAR_DIST = BUCKET_START[-1]


def _params(sem):
    return pltpu.CompilerParams(dimension_semantics=sem, vmem_limit_bytes=VMEM_LIMIT)


TM_NORM = 512
NP_TILES = ROWS_P // TM_NORM


def _rms(x, g):
    ms = jnp.mean(x * x, axis=-1, keepdims=True)
    return x * lax.rsqrt(ms + NORM_EPS) * g


def _norm_kernel(xp_ref, xs_ref, g_ref, o_ref):
    i = pl.program_id(0)

    @pl.when(i < NP_TILES)
    def _():
        o_ref[...] = _rms(xp_ref[...], g_ref[...]).astype(bf16)

    @pl.when(i >= NP_TILES)
    def _():
        o_ref[...] = _rms(xs_ref[...], g_ref[...]).astype(bf16)


def _norm_call(xp, xs, g):
    return pl.pallas_call(
        _norm_kernel,
        grid=(ROWS // TM_NORM,),
        in_specs=[
            pl.BlockSpec((TM_NORM, D_MODEL), lambda i: (jnp.minimum(i, NP_TILES - 1), 0)),
            pl.BlockSpec((TM_NORM, D_MODEL), lambda i: (jnp.maximum(i - NP_TILES, 0), 0)),
            pl.BlockSpec((1, D_MODEL), lambda i: (0, 0)),
        ],
        out_specs=pl.BlockSpec((TM_NORM, D_MODEL), lambda i: (i, 0)),
        out_shape=jax.ShapeDtypeStruct((ROWS, D_MODEL), bf16),
        compiler_params=_params(("arbitrary",)),
        name="norm1",
    )(xp, xs, g)


TM_PROJ = 1088
TM_KVS = 1024
assert ROWS % TM_PROJ == 0 and SEQ % TM_KVS == 0 and ROWS_P % SMALL == 0


def _group_sumsq(z, gmat):
    zz = z * z
    hi = zz.astype(bf16)
    lo = (zz - hi.astype(f32)).astype(bf16)
    parts = []
    for c in range(z.shape[1] // 256):
        sl = slice(256 * c, 256 * (c + 1))
        parts.append(jnp.dot(hi[:, sl], gmat, preferred_element_type=f32)
                     + jnp.dot(lo[:, sl], gmat, preferred_element_type=f32))
    return jnp.concatenate(parts, axis=1)


def _qk_norm(z, g, gmat):
    ss = _group_sumsq(z, gmat)
    return z * lax.rsqrt(ss * (1.0 / HD_DIFF) + NORM_EPS) * g


def _proj_q_kernel(x_ref, w_ref, g_ref, gmat_ref, o_ref):
    j = pl.program_id(0)
    z = jnp.dot(x_ref[...], w_ref[...], preferred_element_type=f32)

    @pl.when(j == 0)
    def _():
        o_ref[...] = (_qk_norm(z, g_ref[...], gmat_ref[...]) * QSCALE).astype(bf16)

    @pl.when(j == 1)
    def _():
        o_ref[...] = (z * QSCALE).astype(bf16)


def _proj_kvd_kernel(x_ref, w_ref, g_ref, gmat_ref, o32_ref, o16_ref):
    j = pl.program_id(0)
    z = jnp.dot(x_ref[...], w_ref[...], preferred_element_type=f32)

    @pl.when(j == 0)
    def _():
        zn = _qk_norm(z, g_ref[...], gmat_ref[...])
        o32_ref[...] = zn
        o16_ref[...] = zn.astype(bf16)

    @pl.when(j > 0)
    def _():
        o32_ref[...] = z
        o16_ref[...] = z.astype(bf16)


def _proj_kvs_kernel(x_ref, w_ref, ot_ref, o16_ref):
    z = jnp.dot(x_ref[...], w_ref[...], preferred_element_type=f32)
    ot_ref[...] = z.T
    o16_ref[...] = z.astype(bf16)


def _proj_gate_kernel(x_ref, w_ref, b_ref, o_ref):
    z = jnp.dot(x_ref[...], w_ref[...], preferred_element_type=f32) + b_ref[...]
    o_ref[...] = (0.5 * jnp.tanh(0.5 * z) + 0.5).astype(bf16)


def _proj_calls(xn, w_in, gq, gk, gmat, b_gate):
    n_rt = ROWS // TM_PROJ
    x_spec = pl.BlockSpec((TM_PROJ, D_MODEL), lambda j, i: (i, 0))
    vec_spec = pl.BlockSpec((1, COL), lambda j, i: (0, 0))
    gmat_spec = pl.BlockSpec((256, 256), lambda j, i: (0, 0))
    out_spec = pl.BlockSpec((TM_PROJ, COL), lambda j, i: (i, j))
    sem = ("arbitrary", "arbitrary")

    q = pl.pallas_call(
        _proj_q_kernel,
        grid=(2, n_rt),
        in_specs=[x_spec, pl.BlockSpec((D_MODEL, COL), lambda j, i: (0, 3 * j)), vec_spec, gmat_spec],
        out_specs=out_spec,
        out_shape=jax.ShapeDtypeStruct((ROWS, 2 * COL), bf16),
        compiler_params=_params(sem),
        name="proj_q",
    )(xn, w_in, gq, gmat)

    kvd32, kvd16 = pl.pallas_call(
        _proj_kvd_kernel,
        grid=(2, n_rt),
        in_specs=[x_spec, pl.BlockSpec((D_MODEL, COL), lambda j, i: (0, j + 1)), vec_spec, gmat_spec],
        out_specs=[out_spec, out_spec],
        out_shape=[jax.ShapeDtypeStruct((ROWS, 2 * COL), f32), jax.ShapeDtypeStruct((ROWS, 2 * COL), bf16)],
        compiler_params=_params(sem),
        name="proj_kvd",
    )(xn, w_in, gk, gmat)

    tps = SEQ // TM_KVS
    kst_p, ks16_p, kst_s, ks16_s = [], [], [], []
    for c in range(2):
        w_spec = pl.BlockSpec((D_MODEL, COL), lambda i, c=c: (0, 4 + c))
        t, r = pl.pallas_call(
            _proj_kvs_kernel,
            grid=(ROWS_P // TM_KVS,),
            in_specs=[pl.BlockSpec((TM_KVS, D_MODEL), lambda i: (i, 0)), w_spec],
            out_specs=[pl.BlockSpec((None, COL, TM_KVS), lambda i: (i // tps, 0, i % tps)),
                       pl.BlockSpec((TM_KVS, COL), lambda i: (i, 0))],
            out_shape=[jax.ShapeDtypeStruct((BATCH, COL, SEQ), f32), jax.ShapeDtypeStruct((ROWS_P, COL), bf16)],
            compiler_params=_params(("arbitrary",)),
            name="proj_kvs",
        )(xn, w_in)
        kst_p.append(t)
        ks16_p.append(r)
        t, r = pl.pallas_call(
            _proj_kvs_kernel,
            grid=(1,),
            in_specs=[pl.BlockSpec((SMALL, D_MODEL), lambda i: (ROWS_P // SMALL, 0)), w_spec],
            out_specs=[pl.BlockSpec((COL, SMALL), lambda i: (0, 0)), pl.BlockSpec((SMALL, COL), lambda i: (0, 0))],
            out_shape=[jax.ShapeDtypeStruct((COL, SMALL), f32), jax.ShapeDtypeStruct((SMALL, COL), bf16)],
            compiler_params=_params(("arbitrary",)),
            name="proj_kvs_small",
        )(xn, w_in)
        kst_s.append(t)
        ks16_s.append(r)

    gates = pl.pallas_call(
        _proj_gate_kernel,
        grid=(4, n_rt),
        in_specs=[x_spec, pl.BlockSpec((D_MODEL, COL), lambda j, i: (0, j + 6)),
                  pl.BlockSpec((1, COL), lambda j, i: (0, j))],
        out_specs=out_spec,
        out_shape=jax.ShapeDtypeStruct((ROWS, 4 * COL), bf16),
        compiler_params=_params(sem),
        name="proj_gate",
    )(xn, w_in, b_gate)
    return q, kvd32, kvd16, kst_p, ks16_p, kst_s, ks16_s, gates


def _nt_dot(a, b):
    return lax.dot_general(a, b, (((1,), (1,)), ((), ())), preferred_element_type=f32)


def _diff_update(parts, m_ref, l_ref, acc_ref, key_axis):
    m_old = m_ref[...]
    m_new = m_old
    for s, _ in parts:
        m_new = jnp.maximum(m_new, jnp.max(s, axis=key_axis, keepdims=True))
    alpha = jnp.exp(m_old - m_new)
    l_new = alpha * l_ref[...]
    acc = alpha * acc_ref[...]
    for s, pv_fn in parts:
        p = jnp.exp(s - m_new)
        l_new = l_new + jnp.sum(p, axis=key_axis, keepdims=True)
        acc = acc + pv_fn(p.astype(bf16))
    l_ref[...] = l_new
    acc_ref[...] = acc
    m_ref[...] = m_new


SIGN_BIT = -2 ** 31
SB_CUTOFF = 105.0


def _sb_update(parts, later_fn, c_ref, acc_ref, key_axis):
    pre = []
    for z, mask, _ in parts:
        neg_abs = lax.bitcast_convert_type(lax.bitcast_convert_type(z, jnp.int32) | SIGN_BIT, f32)
        sp = jnp.maximum(z, 0.0) + jnp.log(1.0 + jnp.exp(neg_abs))
        base = z - sp
        if mask is not None:
            sp = jnp.where(mask, sp, 0.0)
        hi = sp.astype(bf16)
        lo = (sp - hi.astype(f32)).astype(bf16)
        pre.append((base - (later_fn(hi) + later_fn(lo)), jnp.sum(sp, axis=key_axis, keepdims=True)))
    c = c_ref[...]
    acc = acc_ref[...]
    for (base, tot), (_, mask, pv_fn) in zip(pre, parts):
        w = jnp.exp(base - c)
        if mask is not None:
            w = jnp.where(mask, w, 0.0)
        acc = acc + pv_fn(w.astype(bf16))
        c = c + tot
    c_ref[...] = c
    acc_ref[...] = acc


def _bias_chain(dist, rb_of):
    b = jnp.where(dist >= BUCKET_START[1], rb_of(1), rb_of(0))
    for k in range(2, N_BUCKETS):
        b = jnp.where(dist >= BUCKET_START[k], rb_of(k), b)
    return b


def _lam(lq1, lk1, lq2, lk2):
    return (jnp.exp(jnp.sum(lq1 * lk1, axis=1, keepdims=True))
            - jnp.exp(jnp.sum(lq2 * lk2, axis=1, keepdims=True)) + LAM_INIT)


def _sub_norm(od, sg):
    ms = jnp.mean(od * od, axis=-1, keepdims=True)
    return (od * lax.rsqrt(ms + NORM_EPS) * sg) * (1.0 - LAM_INIT)


def _nn(v):
    return lambda p: jnp.dot(p, v, preferred_element_type=f32)


TQ = 256
NQ = SEQ // TQ
META_TILE = 128
assert TQ + 1 >= FAR_DIST and N_META + TQ - (N_META - 1) >= FAR_DIST


def _left(vt):
    return lambda p: jnp.dot(vt, p, preferred_element_type=f32)


def _transpose_bf16(x):
    return x.astype(f32).T.astype(bf16)


def _attn_kernel(rb_ref, qd_ref, qs_ref, kd_ref, vd_ref, ks_ref, vs_ref,
                 kdm_ref, vdm_ref, ksm_ref, vsm_ref,
                 lq1_ref, lk1_ref, lq2_ref, lk2_ref, sg_ref, a_ref, am_ref,
                 od_ref, os_ref,
                 b0_ref, b1_ref, bm_ref, vdt_ref, vst_ref, vdmt_ref, vsmt_ref,
                 md_ref, ld_ref, accd_ref, cs_ref, accs_ref):
    h = pl.program_id(0)
    qi = pl.program_id(2)
    R = 2 * TQ

    def rb_of(k):
        return rb_ref[k, h]

    key = lax.broadcasted_iota(jnp.int32, (TQ, R), 0)
    qry = lax.broadcasted_iota(jnp.int32, (TQ, R), 1) % TQ

    far_bias = rb_of(N_BUCKETS - 1)

    @pl.when(jnp.logical_and(pl.program_id(1) == 0, qi == 0))
    def _():
        b0_ref[...] = _bias_chain(qry - key, rb_of) - far_bias
        b1_ref[...] = _bias_chain(TQ + qry - key, rb_of) - far_bias
        mkey = lax.broadcasted_iota(jnp.int32, (N_META, R), 0)
        mqry = lax.broadcasted_iota(jnp.int32, (N_META, R), 1) % TQ
        bm_ref[...] = _bias_chain(N_META + mqry - mkey, rb_of) - far_bias

    @pl.when(qi == 0)
    def _():
        for j in range(NQ):
            vdt_ref[j] = _transpose_bf16(vd_ref[j * TQ:(j + 1) * TQ, :])
            vst_ref[j] = _transpose_bf16(vs_ref[j * TQ:(j + 1) * TQ, :])
        vdmt_ref[...] = _transpose_bf16(vdm_ref[...])
        vsmt_ref[...] = _transpose_bf16(vsm_ref[...])

    md_ref[...] = jnp.full((1, R), NEG_INF, f32)
    ld_ref[...] = jnp.zeros((1, R), f32)
    accd_ref[...] = jnp.zeros((128, R), f32)
    cs_ref[...] = jnp.zeros((1, R), f32)
    accs_ref[...] = jnp.zeros((128, R), f32)

    lane = lax.broadcasted_iota(jnp.int32, (1, 128), 1)
    lo_half = (lane < HD_DIFF).astype(f32).astype(bf16)
    hi_half = (lane >= HD_DIFF).astype(f32).astype(bf16)
    qd = qd_ref[...]
    qs = qs_ref[...]
    qd2 = jnp.concatenate([qd * lo_half, qd * hi_half], axis=0)
    qs2 = jnp.concatenate([qs * lo_half, qs * hi_half], axis=0)
    a = a_ref[...]

    def later(x):
        return jnp.dot(a, x, preferred_element_type=f32)

    def tiles(specs, with_sb, extra_diff=()):
        dparts = list(extra_diff)
        sparts = []
        for j, bias, dmask, smask in specs:
            sl = pl.ds(pl.multiple_of(j * TQ, TQ), TQ)
            s = _nt_dot(kd_ref[sl, :], qd2)
            if bias is not None:
                s = s + bias
            if dmask is not None:
                s = jnp.where(dmask, s, NEG_INF)
            dparts.append((s, _left(vdt_ref[j])))
            if with_sb:
                sparts.append((_nt_dot(ks_ref[sl, :], qs2), smask, _left(vst_ref[j])))
        _diff_update(dparts, md_ref, ld_ref, accd_ref, 0)
        if with_sb:
            _sb_update(sparts, later, cs_ref, accs_ref, 0)

    def far(js, with_sb):
        tiles([(j, None, None, None) for j in js], with_sb)

    meta_bias = bm_ref[...] * jnp.where(qi == 0, 1.0, 0.0)
    meta_diff = (_nt_dot(kdm_ref[0:N_META, :], qd2) + meta_bias, _left(vdmt_ref[:, 0:N_META]))
    diag = (qi, b0_ref[...], qry >= key, qry > key)

    @pl.when(qi == 0)
    def _():
        tiles([diag], True, [meta_diff])

    @pl.when(qi >= 1)
    def _():
        tiles([diag, (qi - 1, b1_ref[...], None, None)], True, [meta_diff])

    sb_live = jnp.min(cs_ref[...], axis=1, keepdims=True)[0, 0] < SB_CUTOFF
    n_far = jnp.maximum(qi - 1, 0)

    @pl.when(sb_live)
    def _():
        def pair(t, carry):
            j = qi - 2 - 2 * t
            far([j, j - 1], True)
            return carry

        lax.fori_loop(0, n_far // 2, pair, 0)

        @pl.when(n_far % 2 == 1)
        def _():
            far([0], True)

        am = am_ref[...]
        _sb_update([(_nt_dot(ksm_ref[0:N_META, :], qs2), None, _left(vsmt_ref[:, 0:N_META]))],
                   lambda x: jnp.dot(am, x, preferred_element_type=f32), cs_ref, accs_ref, 0)

    @pl.when(jnp.logical_not(sb_live))
    def _():
        def quad(t, carry):
            j = qi - 2 - 4 * t
            far([j, j - 1, j - 2, j - 3], False)
            return carry

        lax.fori_loop(0, n_far // 4, quad, 0)
        rem = n_far % 4

        @pl.when(rem >= 2)
        def _():
            far([rem - 1, rem - 2], False)

        @pl.when(rem % 2 == 1)
        def _():
            far([0], False)

    lam = _lam(lq1_ref[...], lk1_ref[...], lq2_ref[...], lk2_ref[...])
    o = accd_ref[...] / ld_ref[...]
    odt = o[:, :TQ] - lam * o[:, TQ:]
    od_ref[...] = _sub_norm(odt.T, sg_ref[...]).astype(bf16)
    accs = accs_ref[...]
    vrow = lax.broadcasted_iota(jnp.int32, (128, TQ), 0)
    os_ref[...] = jnp.where(vrow < HD_SB, accs[:, :TQ], accs[:, TQ:]).T.astype(bf16)


def _attn_call(rel_bias, q, kvd16, ks16_p, ks16_s, lq1, lk1, lq2, lk2, sg, a, am):
    R = 2 * TQ
    nh = H_DIFF
    assert META_ROW0 % META_TILE == 0 and (META_ROW0 - ROWS_P) % META_TILE == 0

    def qspec(off):
        return pl.BlockSpec((TQ, 128), lambda h, b, i: (b * NQ + i, off + h))

    def kvspec(off):
        return pl.BlockSpec((SEQ, 128), lambda h, b, i: (b, off + h))

    def mspec(off, row0):
        return pl.BlockSpec((META_TILE, 128), lambda h, b, i: (row0 // META_TILE, off + h))

    def const(shape):
        return pl.BlockSpec(shape, lambda h, b, i: (0, 0))

    out_spec = pl.BlockSpec((TQ, 128), lambda h, b, i: (b * NQ + i, h))
    return pl.pallas_call(
        _attn_kernel,
        grid=(nh, BATCH, NQ),
        in_specs=[pl.BlockSpec(memory_space=pltpu.SMEM),
                  qspec(0), qspec(nh),
                  kvspec(0), kvspec(nh), kvspec(0), kvspec(0),
                  mspec(0, META_ROW0), mspec(nh, META_ROW0),
                  mspec(0, META_ROW0 - ROWS_P), mspec(0, META_ROW0 - ROWS_P),
                  const((1, HD_DIFF)), const((1, HD_DIFF)), const((1, HD_DIFF)), const((1, HD_DIFF)),
                  const((1, 128)), const((TQ, TQ)), const((N_META, N_META))],
        out_specs=[out_spec, out_spec],
        out_shape=[jax.ShapeDtypeStruct((ROWS_P, W_DIFF), bf16), jax.ShapeDtypeStruct((ROWS_P, W_SB), bf16)],
        scratch_shapes=[pltpu.VMEM((TQ, R), f32), pltpu.VMEM((TQ, R), f32), pltpu.VMEM((N_META, R), f32),
                        pltpu.VMEM((NQ, 128, TQ), bf16), pltpu.VMEM((NQ, 128, TQ), bf16),
                        pltpu.VMEM((128, META_TILE), bf16), pltpu.VMEM((128, META_TILE), bf16),
                        pltpu.VMEM((1, R), f32), pltpu.VMEM((1, R), f32), pltpu.VMEM((128, R), f32),
                        pltpu.VMEM((1, R), f32), pltpu.VMEM((128, R), f32)],
        compiler_params=_params(("arbitrary", "arbitrary", "arbitrary")),
        name="attn_prompt",
    )(rel_bias, q, q, kvd16, kvd16, ks16_p[0], ks16_p[1], kvd16, kvd16, ks16_s[0], ks16_s[1],
      lq1, lk1, lq2, lk2, sg, a, am)


PPS = 8
PGROUP = 4
NSTEP = N_PAGES // PPS
GPS = PPS // PGROUP
assert N_PAGES % PPS == 0 and PPS % PGROUP == 0 and GPS % 2 == 0
RS = 128
PH = PAGE_SIZE * H_DIFF
assert H_DIFF * 2 * DEC_SEQ == RS and H_SB * DEC_SEQ == RS
assert PAGE_SIZE + 1 >= FAR_DIST


def _diag_blocks_sb(pv):
    return jnp.concatenate([pv[8 * h:8 * (h + 1), 128 * (h // 2):128 * (h // 2 + 1)] for h in range(H_SB)], axis=0)


def _sample_kernel(pt_ref, rbrow_ref, qx_ref, qs_ref, kdn_ref, vdn_ref, ksn_ref, vsn_ref, *rest):
    diff_refs = rest[:2 * PPS]
    (ksb_hbm, vsb_hbm, lq1_ref, lk1_ref, lq2_ref, lk2_ref, sg_ref, u_ref, od_ref, os_ref,
     alast_ref, afar_ref, anear_ref, anew_ref, md_ref, ld_ref, accd_ref, cs_ref, accs_ref,
     ksbuf, vsbuf, sem, live_ref) = rest[2 * PPS:]
    b = pl.program_id(0)
    s = pl.program_id(1)
    qx = qx_ref[...]
    qs = qs_ref[...]

    def sb_copies(bb, group, slot):
        cps = []
        for k in range(PGROUP):
            page = pt_ref[bb, N_PAGES - 1 - (group * PGROUP + k)]
            cps.append(pltpu.make_async_copy(ksb_hbm.at[page], ksbuf.at[slot, k], sem.at[slot]))
            cps.append(pltpu.make_async_copy(vsb_hbm.at[page], vsbuf.at[slot, k], sem.at[slot]))
        return cps

    @pl.when(jnp.logical_and(b == 0, s == 0))
    def _():
        for cp in sb_copies(0, 0, 0):
            cp.start()
        live_ref[0] = 1

    def rb_col(k):
        return rbrow_ref[:, k:k + 1]

    def diff_geometry(n_lanes, base):
        r = lax.broadcasted_iota(jnp.int32, (RS, n_lanes), 0)
        ln = lax.broadcasted_iota(jnp.int32, (RS, n_lanes), 1)
        same_head = (ln % H_DIFF) == (r // (2 * DEC_SEQ))
        dist = base + (r % DEC_SEQ) - (ln // H_DIFF)
        return same_head, dist

    def diff_part(kd, vd, add):
        return (_nt_dot(qx, kd.astype(bf16)) + add, _nn(vd.astype(bf16)))

    u = u_ref[...]

    def later(x):
        return jnp.dot(x, u, preferred_element_type=f32)

    far_bias = rb_col(N_BUCKETS - 1)

    @pl.when(jnp.logical_and(b == 0, s == 0))
    def _():
        same, dist = diff_geometry(DEC_SEQ * H_DIFF, 0)
        anew_ref[...] = jnp.where(same & (dist >= 0), _bias_chain(dist, rb_col) - far_bias, NEG_INF)
        same, dist = diff_geometry(PH, PAGE_SIZE)
        anear_ref[...] = jnp.where(same, _bias_chain(dist, rb_col) - far_bias, NEG_INF)
        afar_ref[...] = jnp.where(same, 0.0, NEG_INF)

    @pl.when(s == 0)
    def _():
        md_ref[...] = jnp.full((RS, 1), NEG_INF, f32)
        ld_ref[...] = jnp.zeros((RS, 1), f32)
        accd_ref[...] = jnp.zeros((RS, 128), f32)
        cs_ref[...] = jnp.zeros((RS, 1), f32)
        accs_ref[...] = jnp.zeros((RS, 128), f32)
        alast_ref[...] = anear_ref[...]
        _diff_update([diff_part(kdn_ref[...], vdn_ref[...], anew_ref[...])], md_ref, ld_ref, accd_ref, 1)
        zpad = jnp.zeros((PAGE_SIZE - DEC_SEQ, COL), f32)
        ksn = jnp.concatenate([ksn_ref[...].astype(f32), zpad], axis=0).astype(bf16)
        vsn = jnp.concatenate([vsn_ref[...].astype(f32), zpad], axis=0).astype(bf16)
        qidx = lax.broadcasted_iota(jnp.int32, (RS, PAGE_SIZE), 0) % DEC_SEQ
        key = lax.broadcasted_iota(jnp.int32, (RS, PAGE_SIZE), 1)
        smask = (qidx > key) & (key < DEC_SEQ)
        _sb_update([(_nt_dot(qs, ksn), smask,
                     lambda w: _diag_blocks_sb(jnp.dot(w, vsn, preferred_element_type=f32)))],
                   later, cs_ref, accs_ref, 1)

    @pl.when(s == 1)
    def _():
        alast_ref[...] = afar_ref[...]

    dparts = []
    for k in range(PPS):
        kd_ref, vd_ref = diff_refs[2 * k:2 * k + 2]
        dparts.append(diff_part(kd_ref[...], vd_ref[...], alast_ref[...] if k == 0 else afar_ref[...]))
    _diff_update(dparts, md_ref, ld_ref, accd_ref, 1)

    for g in range(GPS):
        group = s * GPS + g
        slot = g % 2
        live = live_ref[0] == 1

        @pl.when(live)
        def _():
            for cp in sb_copies(b, group, slot):
                cp.wait()
            sparts = []
            for k in range(PGROUP):
                vt = vsbuf[slot, k].astype(bf16)
                sparts.append((jnp.dot(qs, ksbuf[slot, k].astype(bf16), preferred_element_type=f32), None,
                               lambda w, vt=vt: _diag_blocks_sb(_nt_dot(w, vt))))
            _sb_update(sparts, later, cs_ref, accs_ref, 1)

        still_live = jnp.logical_and(live, jnp.min(cs_ref[...], axis=0, keepdims=True)[0, 0] < SB_CUTOFF)
        if g < GPS - 1:
            next_same_seq = still_live
            next_seq = False
        else:
            last = s == NSTEP - 1
            next_same_seq = jnp.logical_and(jnp.logical_not(last), still_live)
            next_seq = jnp.logical_and(last, b + 1 < DEC_BATCH)

            @pl.when(next_seq)
            def _():
                for cp in sb_copies(b + 1, 0, 0):
                    cp.start()

        @pl.when(next_same_seq)
        def _():
            for cp in sb_copies(b, group + 1, (g + 1) % 2):
                cp.start()

        live_ref[0] = jnp.logical_or(next_same_seq, next_seq).astype(jnp.int32)

    @pl.when(s == NSTEP - 1)
    def _():
        lam = _lam(lq1_ref[...], lk1_ref[...], lq2_ref[...], lk2_ref[...])
        o = accd_ref[...] / ld_ref[...]
        sg = sg_ref[...]
        heads = []
        for h in range(H_DIFF):
            od = o[16 * h:16 * h + 8] - lam * o[16 * h + 8:16 * h + 16]
            heads.append(_sub_norm(od, sg))
        od_ref[...] = jnp.concatenate(heads, axis=1).astype(bf16)
        accs = accs_ref[...]
        lane = lax.broadcasted_iota(jnp.int32, (1, 128), 1)
        pairs = [jnp.where(lane < HD_SB, accs[16 * t:16 * t + 8], accs[16 * t + 8:16 * t + 16])
                 for t in range(H_SB // 2)]
        os_ref[...] = jnp.concatenate(pairs, axis=1).astype(bf16)


def _sample_call(page_table, rbrow, qx, qbd_s, kdn, vdn, ksn, vsn, caches, lq1, lk1, lq2, lk2, sg, u):
    def const2(shape):
        return pl.BlockSpec(shape, lambda b, s, pt: (0, 0))

    def per_b(shape):
        return pl.BlockSpec((None,) + shape, lambda b, s, pt: (b, 0, 0))

    def cache_spec(k):
        return pl.BlockSpec((None, COL, PAGE_SIZE),
                            lambda b, s, pt: (pt[b, N_PAGES - 1 - (s * PPS + k)], 0, 0))

    in_specs = [const2((RS, N_BUCKETS)), per_b((RS, 128)), per_b((RS, COL)),
                per_b((DEC_SEQ * H_DIFF, 128)), per_b((DEC_SEQ * H_DIFF, 128)),
                per_b((DEC_SEQ, COL)), per_b((DEC_SEQ, COL))]
    args = [rbrow, qx, qbd_s, kdn, vdn, ksn, vsn]
    kd_cache, vd_cache, ks_cache, vs_cache = caches
    for k in range(PPS):
        for c in (kd_cache, vd_cache):
            in_specs.append(cache_spec(k))
            args.append(c)
    in_specs += [pl.BlockSpec(memory_space=pl.ANY)] * 2
    args += [ks_cache, vs_cache]
    in_specs += [const2((1, HD_DIFF))] * 4 + [const2((1, 128)), const2((PAGE_SIZE, PAGE_SIZE))]
    args += [lq1, lk1, lq2, lk2, sg, u]
    out_spec = per_b((DEC_SEQ, COL))
    grid_spec = pltpu.PrefetchScalarGridSpec(
        num_scalar_prefetch=1,
        grid=(DEC_BATCH, NSTEP),
        in_specs=in_specs,
        out_specs=[out_spec, out_spec],
        scratch_shapes=[pltpu.VMEM((RS, PH), f32), pltpu.VMEM((RS, PH), f32),
                        pltpu.VMEM((RS, PH), f32), pltpu.VMEM((RS, DEC_SEQ * H_DIFF), f32),
                        pltpu.VMEM((RS, 1), f32), pltpu.VMEM((RS, 1), f32), pltpu.VMEM((RS, 128), f32),
                        pltpu.VMEM((RS, 1), f32), pltpu.VMEM((RS, 128), f32),
                        pltpu.VMEM((2, PGROUP, COL, PAGE_SIZE), f32), pltpu.VMEM((2, PGROUP, COL, PAGE_SIZE), f32),
                        pltpu.SemaphoreType.DMA((2,)), pltpu.SMEM((1,), jnp.int32)],
    )
    return pl.pallas_call(
        _sample_kernel,
        grid_spec=grid_spec,
        out_shape=[jax.ShapeDtypeStruct((DEC_BATCH, DEC_SEQ, W_DIFF), bf16),
                   jax.ShapeDtypeStruct((DEC_BATCH, DEC_SEQ, W_SB), bf16)],
        compiler_params=_params(("arbitrary", "arbitrary")),
        name="attn_sample",
    )(page_table, *args)


TM_MIX = 256
NP_MIX = ROWS_P // TM_MIX


def _mix_kernel(odp_ref, osp_ref, ods_ref, oss_ref, g_ref, xp_ref, xs_ref,
                wa_ref, wb_ref, wo_ref, n2_ref, h2_ref, xn2_ref):
    i = pl.program_id(0)

    def body(od, osb, x):
        ya = jnp.dot(od, wa_ref[...], preferred_element_type=f32)
        yb = jnp.dot(osb, wb_ref[...], preferred_element_type=f32)
        g = g_ref[...]
        mix = g[:, :D_MODEL].astype(f32) * ya + g[:, D_MODEL:].astype(f32) * yb
        h2 = x + jnp.dot(mix.astype(bf16), wo_ref[...], preferred_element_type=f32)
        h2_ref[...] = h2
        xn2_ref[...] = _rms(h2, n2_ref[...]).astype(bf16)

    @pl.when(i < NP_MIX)
    def _():
        body(odp_ref[...], osp_ref[...], xp_ref[...])

    @pl.when(i >= NP_MIX)
    def _():
        body(ods_ref[...], oss_ref[...], xs_ref[...])


def _mix_call(od_p, os_p, od_s, os_s, gates, xp, xs, wa, wb, wo, n2):
    def prow(i):
        return (jnp.minimum(i, NP_MIX - 1), 0)

    def srow(i):
        return (jnp.maximum(i - NP_MIX, 0), 0)

    def full(shape):
        return pl.BlockSpec(shape, lambda i: (0, 0), pipeline_mode=pl.Buffered(1))

    return pl.pallas_call(
        _mix_kernel,
        grid=(ROWS // TM_MIX,),
        in_specs=[pl.BlockSpec((TM_MIX, W_DIFF), prow), pl.BlockSpec((TM_MIX, W_SB), prow),
                  pl.BlockSpec((TM_MIX, W_DIFF), srow), pl.BlockSpec((TM_MIX, W_SB), srow),
                  pl.BlockSpec((TM_MIX, 2 * D_MODEL), lambda i: (i, 0)),
                  pl.BlockSpec((TM_MIX, D_MODEL), prow), pl.BlockSpec((TM_MIX, D_MODEL), srow),
                  full((W_DIFF, D_MODEL)), full((W_SB, D_MODEL)), full((D_MODEL, D_MODEL)),
                  full((1, D_MODEL))],
        out_specs=[pl.BlockSpec((TM_MIX, D_MODEL), lambda i: (i, 0)),
                   pl.BlockSpec((TM_MIX, D_MODEL), lambda i: (i, 0))],
        out_shape=[jax.ShapeDtypeStruct((ROWS, D_MODEL), f32), jax.ShapeDtypeStruct((ROWS, D_MODEL), bf16)],
        compiler_params=_params(("arbitrary",)),
        name="mix_out",
    )(od_p, os_p, od_s, os_s, gates, xp, xs, wa, wb, wo, n2)


TM_MLP = 512
TF_MLP = 1024
assert ROWS_P % TM_MLP == 0 and ROWS_P % SMALL == 0 and D_FF % TF_MLP == 0


def _mlp_kernel(xn_ref, h2_ref, wu_ref, wd_ref, o_ref):
    f = pl.program_id(1)
    u = jnp.maximum(jnp.dot(xn_ref[...], wu_ref[...], preferred_element_type=f32), 0.0)
    part = jnp.dot((u * u).astype(bf16), wd_ref[...], preferred_element_type=f32)

    @pl.when(f == 0)
    def _():
        o_ref[...] = h2_ref[...] + part

    @pl.when(f > 0)
    def _():
        o_ref[...] += part


def _mlp_call(xn2, h2, wu, wd, tm, first_tile, n_tiles, name):
    return pl.pallas_call(
        _mlp_kernel,
        grid=(n_tiles, D_FF // TF_MLP),
        in_specs=[pl.BlockSpec((tm, D_MODEL), lambda i, f: (first_tile + i, 0)),
                  pl.BlockSpec((tm, D_MODEL), lambda i, f: (first_tile + i, 0)),
                  pl.BlockSpec((D_MODEL, TF_MLP), lambda i, f: (0, f)),
                  pl.BlockSpec((TF_MLP, D_MODEL), lambda i, f: (f, 0))],
        out_specs=pl.BlockSpec((tm, D_MODEL), lambda i, f: (i, 0)),
        out_shape=jax.ShapeDtypeStruct((n_tiles * tm, D_MODEL), f32),
        compiler_params=_params(("arbitrary", "arbitrary")),
        name=name,
    )(xn2, h2, wu, wd)


def _later_matrix(n, keys_on_rows=False):
    r = np.arange(n)
    m = r[:, None] > r[None, :]
    return jnp.asarray((m.T if keys_on_rows else m).astype(np.float32), dtype=bf16)


def _block_diag_queries(qrows, n_groups):
    gw = COL // n_groups
    grp = jnp.arange(COL, dtype=jnp.int32) // gw
    keep = grp[None, :] == jnp.arange(n_groups, dtype=jnp.int32)[:, None]
    out = jnp.where(keep[None, :, None, :], qrows[:, None, :, :], jnp.zeros((), qrows.dtype))
    return out.reshape(DEC_BATCH, n_groups * DEC_SEQ, COL)


def kernel(x_prompt, x_sample, cache_k_diff, cache_v_diff, cache_k_sb, cache_v_sb, page_table, meta_tokens,
           rel_bias, norm1_g, w_in, b_gate, qk_norm_q, qk_norm_k, lam_q1, lam_k1, lam_q2, lam_k2, subln_g,
           w_branch_a, w_branch_b, w_o, norm2_g, w_up, w_down):
    l = 0
    xp = x_prompt.reshape(ROWS_P, D_MODEL)
    xs = jnp.concatenate([x_sample.reshape(ROWS_S, D_MODEL), meta_tokens.astype(f32),
                          jnp.zeros((SMALL - ROWS_S - N_META, D_MODEL), f32)], axis=0)
    w_in16 = w_in[l].astype(bf16)
    wa16 = w_branch_a[l].astype(bf16)
    wb16 = w_branch_b[l].astype(bf16)
    wo16 = w_o[l].astype(bf16)
    wu16 = w_up[l].astype(bf16)
    wd16 = w_down[l].astype(bf16)
    gq = jnp.tile(qk_norm_q[l].astype(f32), COL // HD_DIFF)[None]
    gk = jnp.tile(qk_norm_k[l].astype(f32), COL // HD_DIFF)[None]
    grp = np.arange(256) // HD_DIFF
    gmat = jnp.asarray((grp[:, None] == grp[None, :]).astype(np.float32), dtype=bf16)
    lq1, lk1, lq2, lk2 = (a[l].astype(f32)[None] for a in (lam_q1, lam_k1, lam_q2, lam_k2))
    sg = subln_g[l].astype(f32)[None]
    rb = rel_bias.astype(f32)

    xn = _norm_call(xp, xs, norm1_g[l].astype(f32)[None])
    q, kvd32, kvd16, kst_p, ks16_p, kst_s, ks16_s, gates = _proj_calls(xn, w_in16, gq, gk, gmat,
                                                                        b_gate[l].astype(f32)[None])

    od_p, os_p = _attn_call(rb, q, kvd16, ks16_p, ks16_s, lq1, lk1, lq2, lk2, sg,
                            _later_matrix(TQ, keys_on_rows=True), _later_matrix(N_META, keys_on_rows=True))

    q_s = q[ROWS_P:ROWS_P + ROWS_S].reshape(DEC_BATCH, DEC_SEQ, 2 * COL)
    qd_s = jnp.transpose(q_s[..., :COL].reshape(DEC_BATCH, DEC_SEQ, H_DIFF, 2 * HD_DIFF), (0, 2, 1, 3))
    half = (jnp.arange(2 * HD_DIFF, dtype=jnp.int32) // HD_DIFF)[None, :] == jnp.arange(2, dtype=jnp.int32)[:, None]
    qx = jnp.where(half[None, None, :, None, :], qd_s[:, :, None], jnp.zeros((), bf16)).reshape(DEC_BATCH, RS, 128)
    qbd_s = _block_diag_queries(q_s[..., COL:], H_SB)
    new32 = kvd32[ROWS_P:ROWS_P + ROWS_S].reshape(DEC_BATCH, DEC_SEQ, 2 * COL)
    kdn = new32[..., 0:COL].reshape(DEC_BATCH, DEC_SEQ * H_DIFF, 2 * HD_DIFF)
    vdn = new32[..., COL:2 * COL].reshape(DEC_BATCH, DEC_SEQ * H_DIFF, 2 * HD_DIFF)
    ksn = ks16_s[0][:ROWS_S].reshape(DEC_BATCH, DEC_SEQ, COL)
    vsn = ks16_s[1][:ROWS_S].reshape(DEC_BATCH, DEC_SEQ, COL)
    n_phys = cache_k_diff.shape[1]
    caches = [cache_k_diff[l].reshape(n_phys, COL, 2 * HD_DIFF), cache_v_diff[l].reshape(n_phys, COL, 2 * HD_DIFF),
              jnp.transpose(cache_k_sb[l], (0, 2, 3, 1)).reshape(n_phys, COL, PAGE_SIZE),
              jnp.transpose(cache_v_sb[l], (0, 2, 3, 1)).reshape(n_phys, COL, PAGE_SIZE)]
    rbrow = jnp.repeat(rb.T, 2 * DEC_SEQ, axis=0)
    od_s, os_s = _sample_call(page_table, rbrow, qx, qbd_s, kdn, vdn, ksn, vsn, caches, lq1, lk1, lq2, lk2, sg,
                              _later_matrix(PAGE_SIZE))
    pad = jnp.zeros((SMALL - ROWS_S, COL), bf16)
    od_s = jnp.concatenate([od_s.reshape(ROWS_S, COL), pad], axis=0)
    os_s = jnp.concatenate([os_s.reshape(ROWS_S, COL), pad], axis=0)

    h2, xn2 = _mix_call(od_p, os_p, od_s, os_s, gates, xp, xs, wa16, wb16, wo16, norm2_g[l].astype(f32)[None])
    y_p = _mlp_call(xn2, h2, wu16, wd16, TM_MLP, 0, ROWS_P // TM_MLP, "mlp")
    y_s = _mlp_call(xn2, h2, wu16, wd16, ROWS_S, ROWS_P // ROWS_S, 1, "mlp_small")

    y_prompt = y_p.reshape(BATCH, SEQ, D_MODEL)
    y_sample = y_s.reshape(DEC_BATCH, DEC_SEQ, D_MODEL)

    def prompt_kvd(c):
        body = kvd32[:ROWS_P, c * COL:(c + 1) * COL].reshape(BATCH, SEQ, H_DIFF, 2 * HD_DIFF)
        meta = kvd32[META_ROW0:META_ROW0 + N_META, c * COL:(c + 1) * COL].reshape(1, N_META, H_DIFF, 2 * HD_DIFF)
        return jnp.concatenate([jnp.broadcast_to(meta, (BATCH, N_META, H_DIFF, 2 * HD_DIFF)), body], axis=1)[None]

    def sample_kvd(c):
        return kvd32[ROWS_P:ROWS_P + ROWS_S, c * COL:(c + 1) * COL].reshape(1, DEC_BATCH, DEC_SEQ, H_DIFF, 2 * HD_DIFF)

    m0 = META_ROW0 - ROWS_P

    def prompt_kvs(c):
        meta = jnp.broadcast_to(kst_s[c][:, m0:m0 + N_META][None], (BATCH, COL, N_META))
        full = jnp.concatenate([meta, kst_p[c]], axis=2).reshape(BATCH, H_SB, HD_SB, N_META + SEQ)
        return jnp.transpose(full, (0, 3, 1, 2))[None]

    def sample_kvs(c):
        return jnp.transpose(kst_s[c][:, :ROWS_S]).reshape(1, DEC_BATCH, DEC_SEQ, H_SB, HD_SB)

    return (y_prompt, y_sample,
            prompt_kvd(0), prompt_kvd(1), prompt_kvs(0), prompt_kvs(1),
            sample_kvd(0), sample_kvd(1), sample_kvs(0), sample_kvs(1))
```
